```python
import jax, jax.numpy as jnp
from jax import lax
import numpy as np

D_MODEL = 2048
BATCH = 8
SEQ = 2048
DEPTH = 1

GRID_W = 64
ROPE_THETA = 10000.0
Q_BLOCK = 128
RMS_EPS = 1e-6

MLA_HEADS = 8
MLA_Q_RANK = 512
MLA_KV_RANK = 256
MLA_NOPE_DIM = 128
MLA_ROPE_DIM = 64
MLA_V_DIM = 128

GQA_HEADS = 8
GQA_KV_HEADS = 2
GQA_HEAD_DIM = 128
GQA_GROUP = GQA_HEADS // GQA_KV_HEADS

N_EXPERTS = 16
EC_CAPACITY = 2
EXPERT_FF = 1024

IN_WIDTHS = (
    MLA_Q_RANK,
    MLA_KV_RANK,
    MLA_ROPE_DIM,
    GQA_HEADS * GQA_HEAD_DIM,
    GQA_KV_HEADS * GQA_HEAD_DIM,
    GQA_KV_HEADS * GQA_HEAD_DIM,
    D_MODEL,
    D_MODEL,
)
IN_COLS = sum(IN_WIDTHS)
IN_SPLITS = tuple(int(s) for s in np.cumsum(IN_WIDTHS)[:-1])

kernel_name = "hybrid_mla_gqa_axial_ec_moe_encoder"


def _rmsnorm(x, g):
    xf = x.astype(jnp.float32)
    y = xf * lax.rsqrt(jnp.mean(xf * xf, axis=-1, keepdims=True) + RMS_EPS)
    return (y * g.astype(jnp.float32)).astype(x.dtype)


def _rope_angles(pos, dim):
    half = dim // 2
    freqs = ROPE_THETA ** (-jnp.arange(half, dtype=jnp.float32) / half)
    return pos.astype(jnp.float32)[:, None] * freqs[None, :]


def _rope(x, ang):
    cos = jnp.cos(ang).astype(x.dtype)
    sin = jnp.sin(ang).astype(x.dtype)
    x1, x2 = jnp.split(x, 2, axis=-1)
    return jnp.concatenate([x1 * cos - x2 * sin, x1 * sin + x2 * cos], axis=-1)


def _axial_rope(x, ang_row, ang_col):
    xr, xc = jnp.split(x, 2, axis=-1)
    return jnp.concatenate([_rope(xr, ang_row), _rope(xc, ang_col)], axis=-1)


def _query_blocks(q):
    *lead, s, d = q.shape
    qb = q.reshape(*lead, s // Q_BLOCK, Q_BLOCK, d)
    return jnp.moveaxis(qb, -3, 0)


def _merge_blocks(ob):
    ob = jnp.moveaxis(ob, 0, -3)
    return ob.reshape(*ob.shape[:-3], -1, ob.shape[-1])


def _mla_attention(qn, qr, kn, kr, v):
    scale = (MLA_NOPE_DIM + MLA_ROPE_DIM) ** -0.5

    def block(args):
        qn_b, qr_b = args
        s = (jnp.einsum('bhqd,bhkd->bhqk', qn_b, kn, preferred_element_type=jnp.float32)
             + jnp.einsum('bhqr,bkr->bhqk', qr_b, kr, preferred_element_type=jnp.float32)) * scale
        p = jax.nn.softmax(s, axis=-1).astype(v.dtype)
        return jnp.einsum('bhqk,bhkd->bhqd', p, v)

    return _merge_blocks(lax.map(block, (_query_blocks(qn), _query_blocks(qr))))


def _gqa_attention(q, k, v):
    scale = GQA_HEAD_DIM ** -0.5

    def block(q_b):
        s = jnp.einsum('bngqd,bnkd->bngqk', q_b, k, preferred_element_type=jnp.float32) * scale
        p = jax.nn.softmax(s, axis=-1).astype(v.dtype)
        return jnp.einsum('bngqk,bnkd->bngqd', p, v)

    return _merge_blocks(lax.map(block, _query_blocks(q)))


def _expert_choice_ffn(h, w_router, w_e_gate, w_e_up, w_e_down):
    b, s, d = h.shape
    cap = EC_CAPACITY * s // N_EXPERTS
    logits = jnp.einsum('bsd,de->bse', h, w_router, preferred_element_type=jnp.float32)
    aff = jax.nn.softmax(logits, axis=-1)
    vals, idx = lax.top_k(jnp.swapaxes(aff, 1, 2), cap)
    xg = jax.vmap(lambda hb, ib: hb[ib])(h, idx)
    a = jnp.einsum('becd,edf->becf', xg, w_e_gate)
    u = jnp.einsum('becd,edf->becf', xg, w_e_up)
    y = jnp.einsum('becf,efd->becd', jax.nn.silu(a) * u, w_e_down)
    y = y * vals.astype(y.dtype)[..., None]
    return jax.vmap(lambda ib, yb: jnp.zeros((s, d), yb.dtype).at[ib.reshape(-1)].add(yb.reshape(-1, d)))(idx, y)


def _layer(x, c, w_ada, b_ada, g_mix, w_in, g_cq, g_ckv, w_uq, w_ukv, g_qn, g_kn,
           w_br_a, w_br_b, w_out, g_ffn, w_router, w_e_gate, w_e_up, w_e_down,
           ang_1d, ang_row, ang_col):
    b, s, d = x.shape
    mod = jnp.einsum('bd,de->be', jax.nn.silu(c), w_ada) + b_ada
    sh1, sc1, gt1, sh2, sc2, gt2 = jnp.split(mod[:, None, :], 6, axis=-1)

    h = _rmsnorm(x, g_mix) * (1 + sc1) + sh1
    z = h @ w_in
    cq, ckv, kr, qg, kg, vg, gate_a, gate_b = jnp.split(z, IN_SPLITS, axis=-1)

    q = (_rmsnorm(cq, g_cq) @ w_uq).reshape(b, s, MLA_HEADS, MLA_NOPE_DIM + MLA_ROPE_DIM)
    q = q.transpose(0, 2, 1, 3)
    qn, qr = q[..., :MLA_NOPE_DIM], _rope(q[..., MLA_NOPE_DIM:], ang_1d)
    kv = (_rmsnorm(ckv, g_ckv) @ w_ukv).reshape(b, s, MLA_HEADS, MLA_NOPE_DIM + MLA_V_DIM)
    kv = kv.transpose(0, 2, 1, 3)
    kn, va = kv[..., :MLA_NOPE_DIM], kv[..., MLA_NOPE_DIM:]
    kr = _rope(kr, ang_1d)
    o_a = _mla_attention(qn, qr, kn, kr, va)
    o_a = o_a.transpose(0, 2, 1, 3).reshape(b, s, MLA_HEADS * MLA_V_DIM)

    qb = _rmsnorm(qg.reshape(b, s, GQA_HEADS, GQA_HEAD_DIM), g_qn).transpose(0, 2, 1, 3)
    kb = _rmsnorm(kg.reshape(b, s, GQA_KV_HEADS, GQA_HEAD_DIM), g_kn).transpose(0, 2, 1, 3)
    vb = vg.reshape(b, s, GQA_KV_HEADS, GQA_HEAD_DIM).transpose(0, 2, 1, 3)
    qb = _axial_rope(qb, ang_row, ang_col).reshape(b, GQA_KV_HEADS, GQA_GROUP, s, GQA_HEAD_DIM)
    kb = _axial_rope(kb, ang_row, ang_col)
    o_b = _gqa_attention(qb, kb, vb).reshape(b, GQA_HEADS, s, GQA_HEAD_DIM)
    o_b = o_b.transpose(0, 2, 1, 3).reshape(b, s, GQA_HEADS * GQA_HEAD_DIM)

    merged = jax.nn.sigmoid(gate_a) * (o_a @ w_br_a) + jax.nn.sigmoid(gate_b) * (o_b @ w_br_b)
    x = x + gt1 * (merged @ w_out)

    h2 = _rmsnorm(x, g_ffn) * (1 + sc2) + sh2
    x = x + gt2 * _expert_choice_ffn(h2, w_router, w_e_gate, w_e_up, w_e_down)
    return x


def setup_inputs(seed: int = 0) -> dict:
    key = jax.random.key(seed)
    ks = jax.random.split(key, 24)
    f32 = jnp.float32
    L = DEPTH

    def w(k, shape, fan_in):
        return jax.random.normal(k, shape, f32) * (fan_in ** -0.5)

    def gain(k, shape):
        return 1.0 + 0.02 * jax.random.normal(k, shape, f32)

    return {
        "x": jax.random.normal(ks[0], (BATCH, SEQ, D_MODEL), f32),
        "c": jax.random.normal(ks[1], (BATCH, D_MODEL), f32),
        "w_ada": w(ks[2], (L, D_MODEL, 6 * D_MODEL), D_MODEL),
        "b_ada": 0.02 * jax.random.normal(ks[3], (L, 6 * D_MODEL), f32),
        "g_mix": gain(ks[4], (L, D_MODEL)),
        "w_in": w(ks[5], (L, D_MODEL, IN_COLS), D_MODEL),
        "g_cq": gain(ks[6], (L, MLA_Q_RANK)),
        "g_ckv": gain(ks[7], (L, MLA_KV_RANK)),
        "w_uq": w(ks[8], (L, MLA_Q_RANK, MLA_HEADS * (MLA_NOPE_DIM + MLA_ROPE_DIM)), MLA_Q_RANK),
        "w_ukv": w(ks[9], (L, MLA_KV_RANK, MLA_HEADS * (MLA_NOPE_DIM + MLA_V_DIM)), MLA_KV_RANK),
        "g_qn": gain(ks[10], (L, GQA_HEAD_DIM)),
        "g_kn": gain(ks[11], (L, GQA_HEAD_DIM)),
        "w_br_a": w(ks[12], (L, MLA_HEADS * MLA_V_DIM, D_MODEL), MLA_HEADS * MLA_V_DIM),
        "w_br_b": w(ks[13], (L, GQA_HEADS * GQA_HEAD_DIM, D_MODEL), GQA_HEADS * GQA_HEAD_DIM),
        "w_out": w(ks[14], (L, D_MODEL, D_MODEL), D_MODEL),
        "g_ffn": gain(ks[15], (L, D_MODEL)),
        "w_router": w(ks[16], (L, D_MODEL, N_EXPERTS), D_MODEL),
        "w_e_gate": w(ks[17], (L, N_EXPERTS, D_MODEL, EXPERT_FF), D_MODEL),
        "w_e_up": w(ks[18], (L, N_EXPERTS, D_MODEL, EXPERT_FF), D_MODEL),
        "w_e_down": w(ks[19], (L, N_EXPERTS, EXPERT_FF, D_MODEL), EXPERT_FF),
        "g_final": gain(ks[20], (D_MODEL,)),
    }


def reference(x, c, w_ada, b_ada, g_mix, w_in, g_cq, g_ckv, w_uq, w_ukv, g_qn, g_kn,
              w_br_a, w_br_b, w_out, g_ffn, w_router, w_e_gate, w_e_up, w_e_down, g_final):
    s = x.shape[1]
    rows = s // GRID_W
    t = jnp.arange(s)
    row = jnp.repeat(jnp.arange(rows), GRID_W)
    col = jnp.tile(jnp.arange(GRID_W), rows)
    ang_1d = _rope_angles(t, MLA_ROPE_DIM)
    ang_row = _rope_angles(row, GQA_HEAD_DIM // 2)
    ang_col = _rope_angles(col, GQA_HEAD_DIM // 2)
    for l in range(DEPTH):
        x = _layer(x, c, w_ada[l], b_ada[l], g_mix[l], w_in[l], g_cq[l], g_ckv[l], w_uq[l],
                   w_ukv[l], g_qn[l], g_kn[l], w_br_a[l], w_br_b[l], w_out[l], g_ffn[l],
                   w_router[l], w_e_gate[l], w_e_up[l], w_e_down[l], ang_1d, ang_row, ang_col)
    return _rmsnorm(x, g_final)
```

```python
import functools

import jax
import jax.numpy as jnp
from jax import lax
from jax.experimental import pallas as pl
from jax.experimental.pallas import tpu as pltpu

F32 = jnp.float32
BF16 = jnp.bfloat16

GRID_W = 64
ROPE_THETA = 10000.0
RMS_EPS = 1e-6

MLA_HEADS = 8
MLA_Q_RANK = 512
MLA_KV_RANK = 256
MLA_NOPE_DIM = 128
MLA_ROPE_DIM = 64
MLA_V_DIM = 128
MLA_QK_PAD = 256

GQA_HEADS = 8
GQA_KV_HEADS = 2
GQA_HEAD_DIM = 128
GQA_GROUP = GQA_HEADS // GQA_KV_HEADS

N_EXPERTS = 16
EC_CAPACITY = 2

LANES = 128
V7X_VMEM_LIMIT = 56 * 1024 * 1024

_Z_GATE_A = 0
_Z_GATE_B = 2048
_Z_QG = 4096
_Z_KG = 5120
_Z_VG = 5376
_Z_CQ = 5632
_Z_CKV = 6144
_Z_KR = 6400
_Z_COLS = 6656
_Z_TILE = 512


def _params(sem, vmem=V7X_VMEM_LIMIT):
    return pltpu.CompilerParams(dimension_semantics=sem, vmem_limit_bytes=vmem)


def _resident(shape, index_map):
    return pl.BlockSpec(shape, index_map, pipeline_mode=pl.Buffered(1))


def _rms(x):
    return x * lax.rsqrt(jnp.mean(x * x, axis=-1, keepdims=True) + RMS_EPS)


def _rope128(x, cos, sneg, spos):
    return x * cos + pltpu.roll(x, 96, 1) * sneg + pltpu.roll(x, 32, 1) * spos


def _adaln_kernel(c_ref, w_ref, b_ref, o_ref):
    c = c_ref[...]
    sc = (c * jax.nn.sigmoid(c)).astype(BF16)
    o_ref[...] = jnp.dot(sc, w_ref[...].astype(BF16), preferred_element_type=F32) + b_ref[...]


def _adaln(c, w_ada, b_ada):
    b, d = c.shape
    n = w_ada.shape[1]
    tn = 1024
    return pl.pallas_call(
        _adaln_kernel,
        grid=(n // tn,),
        in_specs=[pl.BlockSpec((b, d), lambda j: (0, 0)),
                  pl.BlockSpec((d, tn), lambda j: (0, j)),
                  pl.BlockSpec((1, tn), lambda j: (0, j))],
        out_specs=pl.BlockSpec((b, tn), lambda j: (0, j)),
        out_shape=jax.ShapeDtypeStruct((b, n), F32),
        compiler_params=_params(("arbitrary",)),
        name="adaln",
    )(c, w_ada, b_ada.reshape(1, n))


def _inproj_kernel(x_ref, mod_ref, g_ref, w_ref, z_ref, h_scr):
    @pl.when(pl.program_id(2) == 0)
    def _():
        y = _rms(x_ref[0]) * g_ref[...]
        h_scr[...] = (y * (1.0 + mod_ref[0, 1:2, :]) + mod_ref[0, 0:1, :]).astype(BF16)

    z_ref[0] = jnp.dot(h_scr[...], w_ref[...], preferred_element_type=F32).astype(BF16)


def _inproj(x, mod, g_mix, w_z, tm):
    b, s, d = x.shape
    nj = _Z_COLS // _Z_TILE
    return pl.pallas_call(
        _inproj_kernel,
        grid=(b, s // tm, nj),
        in_specs=[pl.BlockSpec((1, tm, d), lambda bi, i, j: (bi, i, 0)),
                  pl.BlockSpec((1, 6, d), lambda bi, i, j: (bi, 0, 0)),
                  pl.BlockSpec((1, d), lambda bi, i, j: (0, 0)),
                  pl.BlockSpec((d, _Z_TILE), lambda bi, i, j: (0, j))],
        out_specs=pl.BlockSpec((1, tm, _Z_TILE), lambda bi, i, j: (bi, i, j)),
        out_shape=jax.ShapeDtypeStruct((b, s, _Z_COLS), BF16),
        scratch_shapes=[pltpu.VMEM((tm, d), BF16)],
        compiler_params=_params(("arbitrary", "arbitrary", "arbitrary")),
        name="inproj",
    )(x, mod, g_mix.reshape(1, d), w_z)


def _prep_kernel(cq_ref, ckv_ref, kr_ref, qg_ref, kg_ref,
                 gcq_ref, gckv_ref, gqn_ref, gkn_ref, wq_ref, wkv_ref,
                 c1_ref, n1_ref, p1_ref, c2_ref, n2_ref, p2_ref,
                 qm_ref, km_ref, vm_ref, qb_ref, kb_ref):
    c1, n1, p1 = c1_ref[...], n1_ref[...], p1_ref[...]
    c2, n2, p2 = c2_ref[...], n2_ref[...], p2_ref[...]

    cqn = (_rms(cq_ref[0].astype(F32)) * gcq_ref[...]).astype(BF16)
    q = jnp.dot(cqn, wq_ref[...], preferred_element_type=F32)
    q = q * ((MLA_NOPE_DIM + MLA_ROPE_DIM) ** -0.5)
    for h in range(MLA_HEADS):
        lo = h * MLA_QK_PAD
        qm_ref[0, :, lo:lo + LANES] = q[:, lo:lo + LANES].astype(BF16)
        qm_ref[0, :, lo + LANES:lo + 2 * LANES] = _rope128(
            q[:, lo + LANES:lo + 2 * LANES], c1, n1, p1).astype(BF16)

    ckvn = (_rms(ckv_ref[0].astype(F32)) * gckv_ref[...]).astype(BF16)
    kv = jnp.dot(ckvn, wkv_ref[...], preferred_element_type=F32)
    krr = _rope128(kr_ref[0].astype(F32), c1, n1, p1).astype(BF16)
    nk = MLA_HEADS * MLA_NOPE_DIM
    for h in range(MLA_HEADS):
        lo = h * MLA_QK_PAD
        km_ref[0, :, lo:lo + LANES] = kv[:, h * LANES:(h + 1) * LANES].astype(BF16)
        km_ref[0, :, lo + LANES:lo + 2 * LANES] = krr
    vm_ref[0] = kv[:, nk:].astype(BF16)

    gqn = gqn_ref[...] * (GQA_HEAD_DIM ** -0.5)
    for h in range(GQA_HEADS):
        lo = h * GQA_HEAD_DIM
        t = _rms(qg_ref[0, :, lo:lo + GQA_HEAD_DIM].astype(F32)) * gqn
        qb_ref[0, :, lo:lo + GQA_HEAD_DIM] = _rope128(t, c2, n2, p2).astype(BF16)
    for h in range(GQA_KV_HEADS):
        lo = h * GQA_HEAD_DIM
        t = _rms(kg_ref[0, :, lo:lo + GQA_HEAD_DIM].astype(F32)) * gkn_ref[...]
        kb_ref[0, :, lo:lo + GQA_HEAD_DIM] = _rope128(t, c2, n2, p2).astype(BF16)


def _prep(z, g_cq, g_ckv, g_qn, g_kn, wq, wkv, tabs, ts):
    b, s, _ = z.shape

    def zspec(width, col):
        return pl.BlockSpec((1, ts, width), lambda bi, i: (bi, i, col // width))

    def full(a):
        return pl.BlockSpec(a.shape, lambda bi, i: (0,) * a.ndim)

    tab_spec = pl.BlockSpec((ts, LANES), lambda bi, i: (i, 0))
    row = lambda g: g.reshape(1, -1)
    gs = [row(g_cq), row(g_ckv), row(g_qn), row(g_kn)]
    qk = MLA_HEADS * MLA_QK_PAD

    def ospec(width):
        return pl.BlockSpec((1, ts, width), lambda bi, i: (bi, i, 0))

    return pl.pallas_call(
        _prep_kernel,
        grid=(b, s // ts),
        in_specs=[zspec(MLA_Q_RANK, _Z_CQ), zspec(MLA_KV_RANK, _Z_CKV), zspec(LANES, _Z_KR),
                  zspec(GQA_HEADS * GQA_HEAD_DIM, _Z_QG), zspec(GQA_KV_HEADS * GQA_HEAD_DIM, _Z_KG)]
                 + [full(g) for g in gs] + [full(wq), full(wkv)] + [tab_spec] * 6,
        out_specs=[ospec(qk), ospec(qk), ospec(MLA_HEADS * MLA_V_DIM),
                   ospec(GQA_HEADS * GQA_HEAD_DIM), ospec(GQA_KV_HEADS * GQA_HEAD_DIM)],
        out_shape=[jax.ShapeDtypeStruct((b, s, qk), BF16),
                   jax.ShapeDtypeStruct((b, s, qk), BF16),
                   jax.ShapeDtypeStruct((b, s, MLA_HEADS * MLA_V_DIM), BF16),
                   jax.ShapeDtypeStruct((b, s, GQA_HEADS * GQA_HEAD_DIM), BF16),
                   jax.ShapeDtypeStruct((b, s, GQA_KV_HEADS * GQA_HEAD_DIM), BF16)],
        compiler_params=_params(("arbitrary", "arbitrary")),
        name="prep",
    )(z, z, z, z, z, *gs, wq, wkv, *tabs)


def _attn_kernel(q_ref, k_ref, v_ref, o_ref, *, heads, dk, dv):
    k = k_ref[0]
    v = v_ref[0]
    for g in range(heads):
        q = q_ref[0, :, g * dk:(g + 1) * dk]
        s = lax.dot_general(q, k, (((1,), (1,)), ((), ())), preferred_element_type=F32)
        p = jnp.exp(s - jnp.max(s, axis=-1, keepdims=True))
        l = jnp.sum(p, axis=-1, keepdims=True)
        o = jnp.dot(p.astype(BF16), v, preferred_element_type=F32)
        o_ref[0, :, g * dv:(g + 1) * dv] = (o / l).astype(BF16)


def _attention(q, k, v, *, groups, heads, dk, dv, v_block0, tq, name):
    b, s, _ = q.shape
    return pl.pallas_call(
        functools.partial(_attn_kernel, heads=heads, dk=dk, dv=dv),
        grid=(b, groups, s // tq),
        in_specs=[pl.BlockSpec((1, tq, heads * dk), lambda bi, g, i: (bi, i, g)),
                  pl.BlockSpec((1, s, dk), lambda bi, g, i: (bi, 0, g)),
                  pl.BlockSpec((1, s, dv), lambda bi, g, i: (bi, 0, v_block0 + g))],
        out_specs=pl.BlockSpec((1, tq, heads * dv), lambda bi, g, i: (bi, i, g)),
        out_shape=jax.ShapeDtypeStruct((b, s, groups * heads * dv), BF16),
        compiler_params=_params(("arbitrary", "arbitrary", "arbitrary")),
        name=name,
    )(q, k, v)


def _merge_kernel(oa_ref, ob_ref, ga_ref, gb_ref, x_ref, mod_ref, gffn_ref,
                  wa_ref, wb_ref, wo_ref, wrh_ref, wrl_ref,
                  x1_ref, h2_ref, lg_ref):
    a = jnp.dot(oa_ref[0], wa_ref[...], preferred_element_type=F32)
    bb = jnp.dot(ob_ref[0], wb_ref[...], preferred_element_type=F32)
    merged = (jax.nn.sigmoid(ga_ref[0].astype(F32)) * a
              + jax.nn.sigmoid(gb_ref[0].astype(F32)) * bb).astype(BF16)
    y = jnp.dot(merged, wo_ref[...], preferred_element_type=F32)
    x1 = x_ref[0] + mod_ref[0, 2:3, :] * y
    x1_ref[0] = x1
    h2 = _rms(x1) * gffn_ref[...] * (1.0 + mod_ref[0, 4:5, :]) + mod_ref[0, 3:4, :]
    hi = h2.astype(BF16)
    lo = (h2 - hi.astype(F32)).astype(BF16)
    h2_ref[0] = hi
    lg_ref[0] = (jnp.dot(hi, wrh_ref[...], preferred_element_type=F32)
                 + jnp.dot(lo, wrh_ref[...], preferred_element_type=F32)
                 + jnp.dot(hi, wrl_ref[...], preferred_element_type=F32))


def _merge(o_a, o_b, z, x, mod, g_ffn, wa, wb, wo, wrh, wrl, tm):
    b, s, d = x.shape
    na, nb = o_a.shape[2], o_b.shape[2]
    tok = lambda width, col=0: pl.BlockSpec((1, tm, width), lambda bi, i: (bi, i, col // width))
    res = lambda a: _resident(a.shape, lambda bi, i: (0,) * a.ndim)
    return pl.pallas_call(
        _merge_kernel,
        grid=(b, s // tm),
        in_specs=[tok(na), tok(nb), tok(d, _Z_GATE_A), tok(d, _Z_GATE_B), tok(d),
                  pl.BlockSpec((1, 6, d), lambda bi, i: (bi, 0, 0)),
                  pl.BlockSpec((1, d), lambda bi, i: (0, 0)),
                  res(wa), res(wb), res(wo), res(wrh), res(wrl)],
        out_specs=[tok(d), tok(d), tok(LANES)],
        out_shape=[jax.ShapeDtypeStruct((b, s, d), F32),
                   jax.ShapeDtypeStruct((b, s, d), BF16),
                   jax.ShapeDtypeStruct((b, s, LANES), F32)],
        compiler_params=_params(("arbitrary", "arbitrary")),
        name="merge",
    )(o_a, o_b, z, z, x, mod, g_ffn.reshape(1, d), wa, wb, wo, wrh, wrl)


def _lane_cumsum(x):
    n = x.shape[1]
    lane = lax.broadcasted_iota(jnp.int32, x.shape, 1)
    d = 1
    while d < n:
        x = x + jnp.where(lane >= d, pltpu.roll(x, d, 1), 0.0)
        d *= 2
    return x


def _router_kernel(lg_ref, rank_ref, wval_ref, rankc_ref, *, cap):
    e = N_EXPERTS
    lt = jnp.transpose(lg_ref[0])[:e, :]
    m = jnp.max(lt, axis=0, keepdims=True)
    ex = jnp.exp(lt - m)
    aff = ex / jnp.sum(ex, axis=0, keepdims=True)

    def step(_, carry):
        lo, hi = carry
        mid = lo + ((hi - lo) >> 1)
        cnt = jnp.sum((aff >= pltpu.bitcast(mid, F32)).astype(jnp.int32), axis=1, keepdims=True)
        ok = cnt >= cap
        return jnp.where(ok, mid, lo), jnp.where(ok, hi, mid)

    lo0 = jnp.zeros((e, 1), jnp.int32)
    hi0 = jnp.full((e, 1), 0x3F800001, jnp.int32)
    thr_bits, _ = lax.fori_loop(0, 31, step, (lo0, hi0))
    thr = pltpu.bitcast(thr_bits, F32)

    gt = aff > thr
    eq = aff == thr
    need = (cap - jnp.sum(gt.astype(jnp.int32), axis=1, keepdims=True)).astype(F32)
    ceq = _lane_cumsum(eq.astype(F32))
    sel = gt | (eq & (ceq <= need))
    rank = jnp.where(sel, _lane_cumsum(sel.astype(F32)) - 1.0, -1.0)
    rank_ref[0] = rank
    wval_ref[0] = jnp.where(sel, aff, 0.0)
    pad = jnp.full((LANES - e, rank.shape[1]), -1.0, F32)
    rankc_ref[0] = jnp.transpose(jnp.concatenate([rank, pad], axis=0))


def _router(logits, cap):
    b, s, _ = logits.shape
    e = N_EXPERTS
    return pl.pallas_call(
        functools.partial(_router_kernel, cap=cap),
        grid=(b,),
        in_specs=[pl.BlockSpec((1, s, LANES), lambda bi: (bi, 0, 0))],
        out_specs=[pl.BlockSpec((1, e, s), lambda bi: (bi, 0, 0)),
                   pl.BlockSpec((1, e, s), lambda bi: (bi, 0, 0)),
                   pl.BlockSpec((1, s, LANES), lambda bi: (bi, 0, 0))],
        out_shape=[jax.ShapeDtypeStruct((b, e, s), F32),
                   jax.ShapeDtypeStruct((b, e, s), F32),
                   jax.ShapeDtypeStruct((b, s, LANES), F32)],
        compiler_params=_params(("arbitrary",)),
        name="router",
    )(logits)


def _gather_kernel(rank_ref, wval_ref, h2_ref, xg_ref, val_ref, *, cap):
    rank = rank_ref[0, 0]
    s = rank.shape[1]
    hot = rank == lax.broadcasted_iota(jnp.int32, (cap, s), 0).astype(F32)
    xg_ref[0, 0] = jnp.dot(hot.astype(BF16), h2_ref[0], preferred_element_type=F32).astype(BF16)
    val_ref[0, 0] = jnp.sum(jnp.where(hot, wval_ref[0, 0], 0.0), axis=1, keepdims=True)


def _gather(rank, wval, h2, cap):
    b, e, s = rank.shape
    d = h2.shape[2]
    r4 = rank.reshape(b, e, 1, s)
    w4 = wval.reshape(b, e, 1, s)
    row = pl.BlockSpec((1, 1, 1, s), lambda bi, ei: (bi, ei, 0, 0))
    return pl.pallas_call(
        functools.partial(_gather_kernel, cap=cap),
        grid=(b, e),
        in_specs=[row, row, pl.BlockSpec((1, s, d), lambda bi, ei: (bi, 0, 0))],
        out_specs=[pl.BlockSpec((1, 1, cap, d), lambda bi, ei: (bi, ei, 0, 0)),
                   pl.BlockSpec((1, 1, cap, 1), lambda bi, ei: (bi, ei, 0, 0))],
        out_shape=[jax.ShapeDtypeStruct((b, e, cap, d), BF16),
                   jax.ShapeDtypeStruct((b, e, cap, 1), F32)],
        compiler_params=_params(("arbitrary", "arbitrary")),
        name="gather",
    )(r4, w4, h2)


def _expert_kernel(xg_ref, val_ref, wg_ref, wu_ref, wd_ref, y_ref):
    xg = xg_ref[0, 0]
    a = jnp.dot(xg, wg_ref[0], preferred_element_type=F32)
    u = jnp.dot(xg, wu_ref[0], preferred_element_type=F32)
    hm = (a * jax.nn.sigmoid(a) * u).astype(BF16)
    y = jnp.dot(hm, wd_ref[0], preferred_element_type=F32)
    y_ref[0, 0] = (y * val_ref[0, 0]).astype(BF16)


def _experts(xg, val, wg, wu, wd):
    b, e, cap, d = xg.shape
    f = wg.shape[2]
    return pl.pallas_call(
        _expert_kernel,
        grid=(e, b),
        in_specs=[pl.BlockSpec((1, 1, cap, d), lambda ei, bi: (bi, ei, 0, 0)),
                  pl.BlockSpec((1, 1, cap, 1), lambda ei, bi: (bi, ei, 0, 0)),
                  pl.BlockSpec((1, d, f), lambda ei, bi: (ei, 0, 0)),
                  pl.BlockSpec((1, d, f), lambda ei, bi: (ei, 0, 0)),
                  pl.BlockSpec((1, f, d), lambda ei, bi: (ei, 0, 0))],
        out_specs=pl.BlockSpec((1, 1, cap, d), lambda ei, bi: (bi, ei, 0, 0)),
        out_shape=jax.ShapeDtypeStruct((b, e, cap, d), BF16),
        compiler_params=_params(("arbitrary", "arbitrary")),
        name="experts",
    )(xg, val, wg, wu, wd)


def _combine_kernel(rankc_ref, y_ref, x1_ref, mod_ref, gfin_ref, o_ref, *, cap, final):
    tm = rankc_ref.shape[1]
    rc = rankc_ref[0]
    col = lax.broadcasted_iota(jnp.int32, (tm, cap), 1).astype(F32)
    hot = jnp.concatenate(
        [(rc[:, e:e + 1] == col).astype(BF16) for e in range(N_EXPERTS)], axis=1)
    moe = jnp.dot(hot, y_ref[0], preferred_element_type=F32)
    x2 = x1_ref[0] + mod_ref[0, 5:6, :] * moe
    if final:
        x2 = _rms(x2) * gfin_ref[...]
    o_ref[0] = x2


def _combine(rankc, y, x1, mod, g_final, cap, tm, final):
    b, s, d = x1.shape
    y2 = y.reshape(b, N_EXPERTS * cap, d)
    tok = lambda width: pl.BlockSpec((1, tm, width), lambda bi, i: (bi, i, 0))
    return pl.pallas_call(
        functools.partial(_combine_kernel, cap=cap, final=final),
        grid=(b, s // tm),
        in_specs=[tok(LANES),
                  pl.BlockSpec((1, N_EXPERTS * cap, d), lambda bi, i: (bi, 0, 0)),
                  tok(d),
                  pl.BlockSpec((1, 6, d), lambda bi, i: (bi, 0, 0)),
                  pl.BlockSpec((1, d), lambda bi, i: (0, 0))],
        out_specs=tok(d),
        out_shape=jax.ShapeDtypeStruct((b, s, d), F32),
        compiler_params=_params(("arbitrary", "arbitrary")),
        name="combine",
    )(rankc, y2, x1, mod, g_final.reshape(1, d))


def _rope_tables(s):
    half = MLA_ROPE_DIM // 2
    freqs = ROPE_THETA ** (-jnp.arange(half, dtype=F32) / half)
    t = jnp.arange(s)
    ang = lambda pos: pos.astype(F32)[:, None] * freqs[None, :]
    a1, ar, ac = ang(t), ang(t // GRID_W), ang(t % GRID_W)
    z = jnp.zeros((s, half), F32)
    cat = lambda *p: jnp.concatenate(p, axis=1)
    c1 = cat(jnp.cos(a1), jnp.cos(a1), z, z)
    n1 = cat(-jnp.sin(a1), z, z, z)
    p1 = cat(z, jnp.sin(a1), z, z)
    c2 = cat(jnp.cos(ar), jnp.cos(ar), jnp.cos(ac), jnp.cos(ac))
    n2 = cat(-jnp.sin(ar), z, -jnp.sin(ac), z)
    p2 = cat(z, jnp.sin(ar), z, jnp.sin(ac))
    return c1, n1, p1, c2, n2, p2


def _regroup_w_in(w_in):
    d = w_in.shape[0]
    widths = (MLA_Q_RANK, MLA_KV_RANK, MLA_ROPE_DIM, GQA_HEADS * GQA_HEAD_DIM,
              GQA_KV_HEADS * GQA_HEAD_DIM, GQA_KV_HEADS * GQA_HEAD_DIM, d, d)
    offs = [0]
    for w in widths:
        offs.append(offs[-1] + w)
    cq, ckv, kr, qg, kg, vg, ga, gb = [w_in[:, offs[i]:offs[i + 1]] for i in range(8)]
    parts = [ga, gb, qg, kg, vg, cq, ckv, kr]
    used = sum(p.shape[1] for p in parts)
    parts.append(jnp.zeros((d, _Z_COLS - used), w_in.dtype))
    return jnp.concatenate(parts, axis=1).astype(BF16)


def _regroup_w_uq(w_uq):
    r = w_uq.shape[0]
    w = w_uq.reshape(r, MLA_HEADS, MLA_NOPE_DIM + MLA_ROPE_DIM)
    w = jnp.pad(w, ((0, 0), (0, 0), (0, MLA_QK_PAD - MLA_NOPE_DIM - MLA_ROPE_DIM)))
    return w.reshape(r, MLA_HEADS * MLA_QK_PAD).astype(BF16)


def _regroup_w_ukv(w_ukv):
    r = w_ukv.shape[0]
    w = w_ukv.reshape(r, MLA_HEADS, MLA_NOPE_DIM + MLA_V_DIM)
    wk = w[:, :, :MLA_NOPE_DIM].reshape(r, MLA_HEADS * MLA_NOPE_DIM)
    wv = w[:, :, MLA_NOPE_DIM:].reshape(r, MLA_HEADS * MLA_V_DIM)
    return jnp.concatenate([wk, wv], axis=1).astype(BF16)


def _split_router(w_router):
    d, e = w_router.shape
    w = jnp.pad(w_router, ((0, 0), (0, LANES - e)))
    hi = w.astype(BF16)
    lo = (w - hi.astype(F32)).astype(BF16)
    return hi, lo


def kernel(x, c, w_ada, b_ada, g_mix, w_in, g_cq, g_ckv, w_uq, w_ukv, g_qn, g_kn, w_br_a, w_br_b, w_out, g_ffn, w_router, w_e_gate, w_e_up, w_e_down, g_final):
    b, s, d = x.shape
    depth = w_ada.shape[0]
    cap = EC_CAPACITY * s // N_EXPERTS
    assert d == _Z_GATE_B and s % LANES == 0 and cap % LANES == 0
    tm_in = min(1024, s)
    tq = min(256, s)
    tabs = _rope_tables(s)
    for l in range(depth):
        mod = _adaln(c, w_ada[l], b_ada[l]).reshape(b, 6, d)
        z = _inproj(x, mod, g_mix[l], _regroup_w_in(w_in[l]), tm_in)
        qm, km, vm, qb, kb = _prep(z, g_cq[l], g_ckv[l], g_qn[l], g_kn[l],
                                   _regroup_w_uq(w_uq[l]), _regroup_w_ukv(w_ukv[l]), tabs, tq)
        o_a = _attention(qm, km, vm, groups=MLA_HEADS, heads=1, dk=MLA_QK_PAD, dv=MLA_V_DIM,
                         v_block0=0, tq=tq, name="mla_attn")
        o_b = _attention(qb, kb, z, groups=GQA_KV_HEADS, heads=GQA_GROUP, dk=GQA_HEAD_DIM,
                         dv=GQA_HEAD_DIM, v_block0=_Z_VG // GQA_HEAD_DIM, tq=tq, name="gqa_attn")
        wrh, wrl = _split_router(w_router[l])
        x1, h2, logits = _merge(o_a, o_b, z, x, mod, g_ffn[l], w_br_a[l].astype(BF16),
                                w_br_b[l].astype(BF16), w_out[l].astype(BF16), wrh, wrl, tq)
        rank, wval, rankc = _router(logits, cap)
        xg, val = _gather(rank, wval, h2, cap)
        y = _experts(xg, val, w_e_gate[l].astype(BF16), w_e_up[l].astype(BF16),
                     w_e_down[l].astype(BF16))
        x = _combine(rankc, y, x1, mod, g_final, cap, tq, final=(l == depth - 1))
    return x
```

```python
import functools

import jax
import jax.numpy as jnp
from jax import lax
from jax.experimental import pallas as pl
from jax.experimental.pallas import tpu as pltpu

F32 = jnp.float32
BF16 = jnp.bfloat16

GRID_W = 64
ROPE_THETA = 10000.0
RMS_EPS = 1e-6

MLA_HEADS = 8
MLA_Q_RANK = 512
MLA_KV_RANK = 256
MLA_NOPE_DIM = 128
MLA_ROPE_DIM = 64
MLA_V_DIM = 128
MLA_QK_PAD = 256
MLA_HEADS_PER_STEP = 4

GQA_HEADS = 8
GQA_KV_HEADS = 2
GQA_HEAD_DIM = 128
GQA_GROUP = GQA_HEADS // GQA_KV_HEADS

N_EXPERTS = 16
EC_CAPACITY = 2

LANES = 128
V7X_VMEM_LIMIT = 56 * 1024 * 1024

_Z_GATE_A = 0
_Z_GATE_B = 2048
_Z_QG = 4096
_Z_KG = 5120
_Z_VG = 5376
_Z_CQ = 5632
_Z_CKV = 6144
_Z_KR = 6400
_Z_COLS = 6656
_Z_TILE = 512


def _params(sem, vmem=V7X_VMEM_LIMIT):
    return pltpu.CompilerParams(dimension_semantics=sem, vmem_limit_bytes=vmem)


def _resident(shape, index_map):
    return pl.BlockSpec(shape, index_map, pipeline_mode=pl.Buffered(1))


def _rms(x):
    return x * lax.rsqrt(jnp.mean(x * x, axis=-1, keepdims=True) + RMS_EPS)


def _rope128(x, cos, sneg, spos):
    return x * cos + pltpu.roll(x, 96, 1) * sneg + pltpu.roll(x, 32, 1) * spos


def _adaln_kernel(c_ref, w_ref, b_ref, o_ref):
    c = c_ref[...]
    sc = (c * jax.nn.sigmoid(c)).astype(BF16)
    o_ref[...] = jnp.dot(sc, w_ref[...].astype(BF16), preferred_element_type=F32) + b_ref[...]


def _adaln(c, w_ada, b_ada):
    b, d = c.shape
    n = w_ada.shape[1]
    tn = 1024
    return pl.pallas_call(
        _adaln_kernel,
        grid=(n // tn,),
        in_specs=[pl.BlockSpec((b, d), lambda j: (0, 0)),
                  pl.BlockSpec((d, tn), lambda j: (0, j)),
                  pl.BlockSpec((1, tn), lambda j: (0, j))],
        out_specs=pl.BlockSpec((b, tn), lambda j: (0, j)),
        out_shape=jax.ShapeDtypeStruct((b, n), F32),
        compiler_params=_params(("arbitrary",)),
        name="adaln",
    )(c, w_ada, b_ada.reshape(1, n))


def _inproj_kernel(x_ref, mod_ref, g_ref, w_ref, z_ref, h_scr):
    @pl.when(pl.program_id(2) == 0)
    def _():
        y = _rms(x_ref[0]) * g_ref[...]
        h_scr[...] = (y * (1.0 + mod_ref[0, 1:2, :]) + mod_ref[0, 0:1, :]).astype(BF16)

    z_ref[0] = jnp.dot(h_scr[...], w_ref[...], preferred_element_type=F32).astype(BF16)


def _inproj(x, mod, g_mix, w_z, tm):
    b, s, d = x.shape
    nj = _Z_COLS // _Z_TILE
    return pl.pallas_call(
        _inproj_kernel,
        grid=(b, s // tm, nj),
        in_specs=[pl.BlockSpec((1, tm, d), lambda bi, i, j: (bi, i, 0)),
                  pl.BlockSpec((1, 6, d), lambda bi, i, j: (bi, 0, 0)),
                  pl.BlockSpec((1, d), lambda bi, i, j: (0, 0)),
                  pl.BlockSpec((d, _Z_TILE), lambda bi, i, j: (0, j))],
        out_specs=pl.BlockSpec((1, tm, _Z_TILE), lambda bi, i, j: (bi, i, j)),
        out_shape=jax.ShapeDtypeStruct((b, s, _Z_COLS), BF16),
        scratch_shapes=[pltpu.VMEM((tm, d), BF16)],
        compiler_params=_params(("arbitrary", "arbitrary", "arbitrary")),
        name="inproj",
    )(x, mod, g_mix.reshape(1, d), w_z)


def _prep_kernel(cq_ref, ckv_ref, kr_ref, qg_ref, kg_ref,
                 gcq_ref, gckv_ref, gqn_ref, gkn_ref, wq_ref, wkv_ref,
                 c1_ref, n1_ref, p1_ref, c2_ref, n2_ref, p2_ref,
                 qm_ref, km_ref, vm_ref, qb_ref, kb_ref):
    c1, n1, p1 = c1_ref[...], n1_ref[...], p1_ref[...]
    c2, n2, p2 = c2_ref[...], n2_ref[...], p2_ref[...]

    cqn = (_rms(cq_ref[0].astype(F32)) * gcq_ref[...]).astype(BF16)
    q = jnp.dot(cqn, wq_ref[...], preferred_element_type=F32)
    q = q * ((MLA_NOPE_DIM + MLA_ROPE_DIM) ** -0.5)
    for h in range(MLA_HEADS):
        lo = h * MLA_QK_PAD
        qm_ref[0, :, lo:lo + LANES] = q[:, lo:lo + LANES].astype(BF16)
        qm_ref[0, :, lo + LANES:lo + 2 * LANES] = _rope128(
            q[:, lo + LANES:lo + 2 * LANES], c1, n1, p1).astype(BF16)

    ckvn = (_rms(ckv_ref[0].astype(F32)) * gckv_ref[...]).astype(BF16)
    kv = jnp.dot(ckvn, wkv_ref[...], preferred_element_type=F32)
    krr = _rope128(kr_ref[0].astype(F32), c1, n1, p1).astype(BF16)
    nk = MLA_HEADS * MLA_NOPE_DIM
    for h in range(MLA_HEADS):
        lo = h * MLA_QK_PAD
        km_ref[0, :, lo:lo + LANES] = kv[:, h * LANES:(h + 1) * LANES].astype(BF16)
        km_ref[0, :, lo + LANES:lo + 2 * LANES] = krr
    vm_ref[0] = kv[:, nk:].astype(BF16)

    gqn = gqn_ref[...] * (GQA_HEAD_DIM ** -0.5)
    for h in range(GQA_HEADS):
        lo = h * GQA_HEAD_DIM
        t = _rms(qg_ref[0, :, lo:lo + GQA_HEAD_DIM].astype(F32)) * gqn
        qb_ref[0, :, lo:lo + GQA_HEAD_DIM] = _rope128(t, c2, n2, p2).astype(BF16)
    for h in range(GQA_KV_HEADS):
        lo = h * GQA_HEAD_DIM
        t = _rms(kg_ref[0, :, lo:lo + GQA_HEAD_DIM].astype(F32)) * gkn_ref[...]
        kb_ref[0, :, lo:lo + GQA_HEAD_DIM] = _rope128(t, c2, n2, p2).astype(BF16)


def _prep(z, g_cq, g_ckv, g_qn, g_kn, wq, wkv, tabs, ts):
    b, s, _ = z.shape

    def zspec(width, col):
        return pl.BlockSpec((1, ts, width), lambda bi, i: (bi, i, col // width))

    def full(a):
        return pl.BlockSpec(a.shape, lambda bi, i: (0,) * a.ndim)

    tab_spec = pl.BlockSpec((ts, LANES), lambda bi, i: (i, 0))
    row = lambda g: g.reshape(1, -1)
    gs = [row(g_cq), row(g_ckv), row(g_qn), row(g_kn)]
    qk = MLA_HEADS * MLA_QK_PAD

    def ospec(width):
        return pl.BlockSpec((1, ts, width), lambda bi, i: (bi, i, 0))

    return pl.pallas_call(
        _prep_kernel,
        grid=(b, s // ts),
        in_specs=[zspec(MLA_Q_RANK, _Z_CQ), zspec(MLA_KV_RANK, _Z_CKV), zspec(LANES, _Z_KR),
                  zspec(GQA_HEADS * GQA_HEAD_DIM, _Z_QG), zspec(GQA_KV_HEADS * GQA_HEAD_DIM, _Z_KG)]
                 + [full(g) for g in gs] + [full(wq), full(wkv)] + [tab_spec] * 6,
        out_specs=[ospec(qk), ospec(qk), ospec(MLA_HEADS * MLA_V_DIM),
                   ospec(GQA_HEADS * GQA_HEAD_DIM), ospec(GQA_KV_HEADS * GQA_HEAD_DIM)],
        out_shape=[jax.ShapeDtypeStruct((b, s, qk), BF16),
                   jax.ShapeDtypeStruct((b, s, qk), BF16),
                   jax.ShapeDtypeStruct((b, s, MLA_HEADS * MLA_V_DIM), BF16),
                   jax.ShapeDtypeStruct((b, s, GQA_HEADS * GQA_HEAD_DIM), BF16),
                   jax.ShapeDtypeStruct((b, s, GQA_KV_HEADS * GQA_HEAD_DIM), BF16)],
        compiler_params=_params(("arbitrary", "arbitrary")),
        name="prep",
    )(z, z, z, z, z, *gs, wq, wkv, *tabs)


def _attn_kernel(q_ref, k_ref, v_ref, o_ref, *, heads, dk, dv, shared_kv):
    for g in range(heads):
        kv = 0 if shared_kv else g
        q = q_ref[0, :, g * dk:(g + 1) * dk]
        k = k_ref[0, :, kv * dk:(kv + 1) * dk]
        v = v_ref[0, :, kv * dv:(kv + 1) * dv]
        s = lax.dot_general(q, k, (((1,), (1,)), ((), ())), preferred_element_type=F32)
        p = jnp.exp(s - jnp.max(s, axis=-1, keepdims=True))
        l = jnp.sum(p, axis=-1, keepdims=True)
        o = jnp.dot(p.astype(BF16), v, preferred_element_type=F32)
        o_ref[0, :, g * dv:(g + 1) * dv] = (o / l).astype(BF16)


def _attention(q, k, v, *, groups, heads, dk, dv, shared_kv, v_block0, tq, name):
    b, s, _ = q.shape
    kvh = 1 if shared_kv else heads
    return pl.pallas_call(
        functools.partial(_attn_kernel, heads=heads, dk=dk, dv=dv, shared_kv=shared_kv),
        grid=(b, groups, s // tq),
        in_specs=[pl.BlockSpec((1, tq, heads * dk), lambda bi, g, i: (bi, i, g)),
                  pl.BlockSpec((1, s, kvh * dk), lambda bi, g, i: (bi, 0, g)),
                  pl.BlockSpec((1, s, kvh * dv), lambda bi, g, i: (bi, 0, v_block0 + g))],
        out_specs=pl.BlockSpec((1, tq, heads * dv), lambda bi, g, i: (bi, i, g)),
        out_shape=jax.ShapeDtypeStruct((b, s, groups * heads * dv), BF16),
        compiler_params=_params(("arbitrary", "arbitrary", "arbitrary")),
        name=name,
    )(q, k, v)


def _merge_kernel(oa_ref, ob_ref, ga_ref, gb_ref, x_ref, mod_ref, gffn_ref,
                  wa_ref, wb_ref, wo_ref, wrh_ref, wrl_ref,
                  x1_ref, h2_ref, lg_ref):
    a = jnp.dot(oa_ref[0], wa_ref[...], preferred_element_type=F32)
    bb = jnp.dot(ob_ref[0], wb_ref[...], preferred_element_type=F32)
    merged = (jax.nn.sigmoid(ga_ref[0].astype(F32)) * a
              + jax.nn.sigmoid(gb_ref[0].astype(F32)) * bb).astype(BF16)
    y = jnp.dot(merged, wo_ref[...], preferred_element_type=F32)
    x1 = x_ref[0] + mod_ref[0, 2:3, :] * y
    x1_ref[0] = x1
    h2 = _rms(x1) * gffn_ref[...] * (1.0 + mod_ref[0, 4:5, :]) + mod_ref[0, 3:4, :]
    hi = h2.astype(BF16)
    lo = (h2 - hi.astype(F32)).astype(BF16)
    h2_ref[0] = hi
    lg_ref[0] = (jnp.dot(hi, wrh_ref[...], preferred_element_type=F32)
                 + jnp.dot(lo, wrh_ref[...], preferred_element_type=F32)
                 + jnp.dot(hi, wrl_ref[...], preferred_element_type=F32))


def _merge(o_a, o_b, z, x, mod, g_ffn, wa, wb, wo, wrh, wrl, tm):
    b, s, d = x.shape
    na, nb = o_a.shape[2], o_b.shape[2]
    tok = lambda width, col=0: pl.BlockSpec((1, tm, width), lambda bi, i: (bi, i, col // width))
    res = lambda a: _resident(a.shape, lambda bi, i: (0,) * a.ndim)
    return pl.pallas_call(
        _merge_kernel,
        grid=(b, s // tm),
        in_specs=[tok(na), tok(nb), tok(d, _Z_GATE_A), tok(d, _Z_GATE_B), tok(d),
                  pl.BlockSpec((1, 6, d), lambda bi, i: (bi, 0, 0)),
                  pl.BlockSpec((1, d), lambda bi, i: (0, 0)),
                  res(wa), res(wb), res(wo), res(wrh), res(wrl)],
        out_specs=[tok(d), tok(d), tok(LANES)],
        out_shape=[jax.ShapeDtypeStruct((b, s, d), F32),
                   jax.ShapeDtypeStruct((b, s, d), BF16),
                   jax.ShapeDtypeStruct((b, s, LANES), F32)],
        compiler_params=_params(("arbitrary", "arbitrary")),
        name="merge",
    )(o_a, o_b, z, z, x, mod, g_ffn.reshape(1, d), wa, wb, wo, wrh, wrl)


def _lane_cumsum(x):
    n = x.shape[1]
    lane = lax.broadcasted_iota(jnp.int32, x.shape, 1)
    d = 1
    while d < n:
        x = x + jnp.where(lane >= d, pltpu.roll(x, d, 1), 0.0)
        d *= 2
    return x


def _router_kernel(lg_ref, rank_ref, wval_ref, rankc_ref, *, cap):
    e = N_EXPERTS
    lt = jnp.transpose(lg_ref[0])[:e, :]
    m = jnp.max(lt, axis=0, keepdims=True)
    ex = jnp.exp(lt - m)
    aff = ex / jnp.sum(ex, axis=0, keepdims=True)

    def step(_, carry):
        lo, hi = carry
        mid = lo + ((hi - lo) >> 1)
        cnt = jnp.sum((aff >= pltpu.bitcast(mid, F32)).astype(jnp.int32), axis=1, keepdims=True)
        ok = cnt >= cap
        return jnp.where(ok, mid, lo), jnp.where(ok, hi, mid)

    lo0 = jnp.zeros((e, 1), jnp.int32)
    hi0 = jnp.full((e, 1), 0x3F800001, jnp.int32)
    thr_bits, _ = lax.fori_loop(0, 31, step, (lo0, hi0))
    thr = pltpu.bitcast(thr_bits, F32)

    gt = aff > thr
    eq = aff == thr
    need = (cap - jnp.sum(gt.astype(jnp.int32), axis=1, keepdims=True)).astype(F32)
    ceq = _lane_cumsum(eq.astype(F32))
    sel = gt | (eq & (ceq <= need))
    rank = jnp.where(sel, _lane_cumsum(sel.astype(F32)) - 1.0, -1.0)
    rank_ref[0] = rank
    wval_ref[0] = jnp.where(sel, aff, 0.0)
    pad = jnp.full((LANES - e, rank.shape[1]), -1.0, F32)
    rankc_ref[0] = jnp.transpose(jnp.concatenate([rank, pad], axis=0))


def _router(logits, cap):
    b, s, _ = logits.shape
    e = N_EXPERTS
    return pl.pallas_call(
        functools.partial(_router_kernel, cap=cap),
        grid=(b,),
        in_specs=[pl.BlockSpec((1, s, LANES), lambda bi: (bi, 0, 0))],
        out_specs=[pl.BlockSpec((1, e, s), lambda bi: (bi, 0, 0)),
                   pl.BlockSpec((1, e, s), lambda bi: (bi, 0, 0)),
                   pl.BlockSpec((1, s, LANES), lambda bi: (bi, 0, 0))],
        out_shape=[jax.ShapeDtypeStruct((b, e, s), F32),
                   jax.ShapeDtypeStruct((b, e, s), F32),
                   jax.ShapeDtypeStruct((b, s, LANES), F32)],
        compiler_params=_params(("arbitrary",)),
        name="router",
    )(logits)


def _gather_kernel(rank_ref, wval_ref, h2_ref, xg_ref, val_ref, *, cap):
    rank = rank_ref[0, 0]
    s = rank.shape[1]
    hot = rank == lax.broadcasted_iota(jnp.int32, (cap, s), 0).astype(F32)
    xg_ref[0, 0] = jnp.dot(hot.astype(BF16), h2_ref[0], preferred_element_type=F32).astype(BF16)
    val_ref[0, 0] = jnp.sum(jnp.where(hot, wval_ref[0, 0], 0.0), axis=1, keepdims=True)


def _gather(rank, wval, h2, cap):
    b, e, s = rank.shape
    d = h2.shape[2]
    r4 = rank.reshape(b, e, 1, s)
    w4 = wval.reshape(b, e, 1, s)
    row = pl.BlockSpec((1, 1, 1, s), lambda bi, ei: (bi, ei, 0, 0))
    return pl.pallas_call(
        functools.partial(_gather_kernel, cap=cap),
        grid=(b, e),
        in_specs=[row, row, pl.BlockSpec((1, s, d), lambda bi, ei: (bi, 0, 0))],
        out_specs=[pl.BlockSpec((1, 1, cap, d), lambda bi, ei: (bi, ei, 0, 0)),
                   pl.BlockSpec((1, 1, cap, 1), lambda bi, ei: (bi, ei, 0, 0))],
        out_shape=[jax.ShapeDtypeStruct((b, e, cap, d), BF16),
                   jax.ShapeDtypeStruct((b, e, cap, 1), F32)],
        compiler_params=_params(("arbitrary", "arbitrary")),
        name="gather",
    )(r4, w4, h2)


def _expert_kernel(xg_ref, val_ref, wg_ref, wu_ref, wd_ref, y_ref, wg_s, wu_s, wd_s, *, n_exp, n_b):
    p = pl.program_id(0)
    bi = pl.program_id(1)
    ng = n_b // 2
    rows = wd_ref.shape[1]
    stage = p % 2
    ready = 1 - stage

    @pl.when(p < n_exp)
    def _():
        @pl.when(bi < ng)
        def _():
            wg_s[stage, bi] = wg_ref[0].astype(BF16)

        @pl.when(bi >= ng)
        def _():
            wu_s[stage, bi - ng] = wu_ref[0].astype(BF16)

        wd_s[stage, pl.ds(pl.multiple_of(bi * rows, rows), rows), :] = wd_ref[0].astype(BF16)

    @pl.when(p > 0)
    def _():
        xg = xg_ref[0, 0]
        a = jnp.concatenate([jnp.dot(xg, wg_s[ready, c], preferred_element_type=F32)
                             for c in range(ng)], axis=1)
        u = jnp.concatenate([jnp.dot(xg, wu_s[ready, c], preferred_element_type=F32)
                             for c in range(ng)], axis=1)
        hm = (a * jax.nn.sigmoid(a) * u).astype(BF16)
        y = jnp.dot(hm, wd_s[ready], preferred_element_type=F32)
        y_ref[0, 0] = (y * val_ref[0, 0]).astype(BF16)


def _experts(xg, val, wg, wu, wd):
    b, e, cap, d = xg.shape
    f = wg.shape[2]
    assert b % 2 == 0 and f % (b // 2) == 0 and f % b == 0
    ng = b // 2
    fc, fr = f // ng, f // b
    last = e - 1
    tok = lambda p, bi: (jnp.where(p == 0, 0, bi), jnp.maximum(p - 1, 0), 0, 0)
    exp = lambda p: jnp.minimum(p, last)
    return pl.pallas_call(
        functools.partial(_expert_kernel, n_exp=e, n_b=b),
        grid=(e + 1, b),
        in_specs=[pl.BlockSpec((1, 1, cap, d), tok),
                  pl.BlockSpec((1, 1, cap, 1), tok),
                  pl.BlockSpec((1, d, fc), lambda p, bi: (
                      exp(p), 0, jnp.where(p <= last, jnp.minimum(bi, ng - 1), ng - 1))),
                  pl.BlockSpec((1, d, fc), lambda p, bi: (
                      exp(p), 0, jnp.where(p <= last, jnp.maximum(bi - ng, 0), ng - 1))),
                  pl.BlockSpec((1, fr, d), lambda p, bi: (
                      exp(p), jnp.where(p <= last, bi, b - 1), 0))],
        out_specs=pl.BlockSpec((1, 1, cap, d), tok),
        out_shape=jax.ShapeDtypeStruct((b, e, cap, d), BF16),
        scratch_shapes=[pltpu.VMEM((2, ng, d, fc), BF16),
                        pltpu.VMEM((2, ng, d, fc), BF16),
                        pltpu.VMEM((2, f, d), BF16)],
        compiler_params=_params(("arbitrary", "arbitrary")),
        name="experts",
    )(xg, val, wg, wu, wd)


def _combine_kernel(rankc_ref, y_ref, x1_ref, mod_ref, gfin_ref, o_ref, *, cap, final):
    tm = rankc_ref.shape[1]
    rc = rankc_ref[0]
    col = lax.broadcasted_iota(jnp.int32, (tm, cap), 1).astype(F32)
    hot = jnp.concatenate(
        [(rc[:, e:e + 1] == col).astype(BF16) for e in range(N_EXPERTS)], axis=1)
    moe = jnp.dot(hot, y_ref[0], preferred_element_type=F32)
    x2 = x1_ref[0] + mod_ref[0, 5:6, :] * moe
    if final:
        x2 = _rms(x2) * gfin_ref[...]
    o_ref[0] = x2


def _combine(rankc, y, x1, mod, g_final, cap, tm, final):
    b, s, d = x1.shape
    y2 = y.reshape(b, N_EXPERTS * cap, d)
    tok = lambda width: pl.BlockSpec((1, tm, width), lambda bi, i: (bi, i, 0))
    return pl.pallas_call(
        functools.partial(_combine_kernel, cap=cap, final=final),
        grid=(b, s // tm),
        in_specs=[tok(LANES),
                  pl.BlockSpec((1, N_EXPERTS * cap, d), lambda bi, i: (bi, 0, 0)),
                  tok(d),
                  pl.BlockSpec((1, 6, d), lambda bi, i: (bi, 0, 0)),
                  pl.BlockSpec((1, d), lambda bi, i: (0, 0))],
        out_specs=tok(d),
        out_shape=jax.ShapeDtypeStruct((b, s, d), F32),
        compiler_params=_params(("arbitrary", "arbitrary")),
        name="combine",
    )(rankc, y2, x1, mod, g_final.reshape(1, d))


def _rope_tables(s):
    half = MLA_ROPE_DIM // 2
    freqs = ROPE_THETA ** (-jnp.arange(half, dtype=F32) / half)
    t = jnp.arange(s)
    ang = lambda pos: pos.astype(F32)[:, None] * freqs[None, :]
    a1, ar, ac = ang(t), ang(t // GRID_W), ang(t % GRID_W)
    z = jnp.zeros((s, half), F32)
    cat = lambda *p: jnp.concatenate(p, axis=1)
    c1 = cat(jnp.cos(a1), jnp.cos(a1), z, z)
    n1 = cat(-jnp.sin(a1), z, z, z)
    p1 = cat(z, jnp.sin(a1), z, z)
    c2 = cat(jnp.cos(ar), jnp.cos(ar), jnp.cos(ac), jnp.cos(ac))
    n2 = cat(-jnp.sin(ar), z, -jnp.sin(ac), z)
    p2 = cat(z, jnp.sin(ar), z, jnp.sin(ac))
    return c1, n1, p1, c2, n2, p2


def _regroup_w_in(w_in):
    d = w_in.shape[0]
    widths = (MLA_Q_RANK, MLA_KV_RANK, MLA_ROPE_DIM, GQA_HEADS * GQA_HEAD_DIM,
              GQA_KV_HEADS * GQA_HEAD_DIM, GQA_KV_HEADS * GQA_HEAD_DIM, d, d)
    offs = [0]
    for w in widths:
        offs.append(offs[-1] + w)
    cq, ckv, kr, qg, kg, vg, ga, gb = [w_in[:, offs[i]:offs[i + 1]] for i in range(8)]
    parts = [ga, gb, qg, kg, vg, cq, ckv, kr]
    used = sum(p.shape[1] for p in parts)
    parts.append(jnp.zeros((d, _Z_COLS - used), w_in.dtype))
    return jnp.concatenate(parts, axis=1).astype(BF16)


def _regroup_w_uq(w_uq):
    r = w_uq.shape[0]
    w = w_uq.reshape(r, MLA_HEADS, MLA_NOPE_DIM + MLA_ROPE_DIM)
    w = jnp.pad(w, ((0, 0), (0, 0), (0, MLA_QK_PAD - MLA_NOPE_DIM - MLA_ROPE_DIM)))
    return w.reshape(r, MLA_HEADS * MLA_QK_PAD).astype(BF16)


def _regroup_w_ukv(w_ukv):
    r = w_ukv.shape[0]
    w = w_ukv.reshape(r, MLA_HEADS, MLA_NOPE_DIM + MLA_V_DIM)
    wk = w[:, :, :MLA_NOPE_DIM].reshape(r, MLA_HEADS * MLA_NOPE_DIM)
    wv = w[:, :, MLA_NOPE_DIM:].reshape(r, MLA_HEADS * MLA_V_DIM)
    return jnp.concatenate([wk, wv], axis=1).astype(BF16)


def _split_router(w_router):
    d, e = w_router.shape
    w = jnp.pad(w_router, ((0, 0), (0, LANES - e)))
    hi = w.astype(BF16)
    lo = (w - hi.astype(F32)).astype(BF16)
    return hi, lo


def kernel(x, c, w_ada, b_ada, g_mix, w_in, g_cq, g_ckv, w_uq, w_ukv, g_qn, g_kn, w_br_a, w_br_b, w_out, g_ffn, w_router, w_e_gate, w_e_up, w_e_down, g_final):
    b, s, d = x.shape
    depth = w_ada.shape[0]
    cap = EC_CAPACITY * s // N_EXPERTS
    assert d == _Z_GATE_B and s % LANES == 0 and cap % LANES == 0
    tm_in = min(1024, s)
    tq = min(256, s)
    tabs = _rope_tables(s)
    for l in range(depth):
        mod = _adaln(c, w_ada[l], b_ada[l]).reshape(b, 6, d)
        z = _inproj(x, mod, g_mix[l], _regroup_w_in(w_in[l]), tm_in)
        qm, km, vm, qb, kb = _prep(z, g_cq[l], g_ckv[l], g_qn[l], g_kn[l],
                                   _regroup_w_uq(w_uq[l]), _regroup_w_ukv(w_ukv[l]), tabs, tq)
        o_a = _attention(qm, km, vm, groups=MLA_HEADS // MLA_HEADS_PER_STEP,
                         heads=MLA_HEADS_PER_STEP, dk=MLA_QK_PAD, dv=MLA_V_DIM, shared_kv=False,
                         v_block0=0, tq=tq, name="mla_attn")
        o_b = _attention(qb, kb, z, groups=GQA_KV_HEADS, heads=GQA_GROUP, dk=GQA_HEAD_DIM,
                         dv=GQA_HEAD_DIM, shared_kv=True, v_block0=_Z_VG // GQA_HEAD_DIM, tq=tq,
                         name="gqa_attn")
        wrh, wrl = _split_router(w_router[l])
        x1, h2, logits = _merge(o_a, o_b, z, x, mod, g_ffn[l], w_br_a[l].astype(BF16),
                                w_br_b[l].astype(BF16), w_out[l].astype(BF16), wrh, wrl, tq)
        rank, wval, rankc = _router(logits, cap)
        xg, val = _gather(rank, wval, h2, cap)
        y = _experts(xg, val, w_e_gate[l], w_e_up[l], w_e_down[l])
        x = _combine(rankc, y, x1, mod, g_final, cap, tq, final=(l == depth - 1))
    return x
```

```python
import functools

import jax
import jax.numpy as jnp
from jax import lax
from jax.experimental import pallas as pl
from jax.experimental.pallas import tpu as pltpu

F32 = jnp.float32
BF16 = jnp.bfloat16

GRID_W = 64
ROPE_THETA = 10000.0
RMS_EPS = 1e-6

MLA_HEADS = 8
MLA_Q_RANK = 512
MLA_KV_RANK = 256
MLA_NOPE_DIM = 128
MLA_ROPE_DIM = 64
MLA_V_DIM = 128
MLA_QK_PAD = 256
MLA_HEADS_PER_STEP = 4
ATTN_KEY_CHUNKS = 4

GQA_HEADS = 8
GQA_KV_HEADS = 2
GQA_HEAD_DIM = 128
GQA_GROUP = GQA_HEADS // GQA_KV_HEADS

N_EXPERTS = 16
EC_CAPACITY = 2

LANES = 128
V7X_VMEM_LIMIT = 56 * 1024 * 1024

_Z_GATE_A = 0
_Z_GATE_B = 2048
_Z_QG = 4096
_Z_KG = 5120
_Z_VG = 5376
_Z_CQ = 5632
_Z_CKV = 6144
_Z_KR = 6400
_Z_COLS = 6656
_Z_TILE = 512


def _params(sem, vmem=V7X_VMEM_LIMIT):
    return pltpu.CompilerParams(dimension_semantics=sem, vmem_limit_bytes=vmem)


def _resident(shape, index_map):
    return pl.BlockSpec(shape, index_map, pipeline_mode=pl.Buffered(1))


def _rms(x):
    return x * lax.rsqrt(jnp.mean(x * x, axis=-1, keepdims=True) + RMS_EPS)


def _rope128(x, cos, sin):
    return x * cos + pltpu.roll(x, LANES // 2, 1) * sin


def _adaln_kernel(c_ref, w_ref, b_ref, o_ref):
    c = c_ref[...]
    sc = (c * jax.nn.sigmoid(c)).astype(BF16)
    o_ref[...] = jnp.dot(sc, w_ref[...].astype(BF16), preferred_element_type=F32) + b_ref[...]


def _adaln(c, w_ada, b_ada):
    b, d = c.shape
    n = w_ada.shape[1]
    tn = 1024
    return pl.pallas_call(
        _adaln_kernel,
        grid=(n // tn,),
        in_specs=[pl.BlockSpec((b, d), lambda j: (0, 0)),
                  pl.BlockSpec((d, tn), lambda j: (0, j)),
                  pl.BlockSpec((1, tn), lambda j: (0, j))],
        out_specs=pl.BlockSpec((b, tn), lambda j: (0, j)),
        out_shape=jax.ShapeDtypeStruct((b, n), F32),
        compiler_params=_params(("arbitrary",)),
        name="adaln",
    )(c, w_ada, b_ada.reshape(1, n))


def _inproj_kernel(x_ref, mod_ref, g_ref, w_ref, z_ref, h_scr):
    @pl.when(pl.program_id(2) == 0)
    def _():
        y = _rms(x_ref[0]) * g_ref[...]
        h_scr[...] = (y * (1.0 + mod_ref[0, 1:2, :]) + mod_ref[0, 0:1, :]).astype(BF16)

    z_ref[0] = jnp.dot(h_scr[...], w_ref[...], preferred_element_type=F32).astype(BF16)


def _inproj(x, mod, g_mix, w_z, tm):
    b, s, d = x.shape
    nj = _Z_COLS // _Z_TILE
    return pl.pallas_call(
        _inproj_kernel,
        grid=(b, s // tm, nj),
        in_specs=[pl.BlockSpec((1, tm, d), lambda bi, i, j: (bi, i, 0)),
                  pl.BlockSpec((1, 6, d), lambda bi, i, j: (bi, 0, 0)),
                  pl.BlockSpec((1, d), lambda bi, i, j: (0, 0)),
                  pl.BlockSpec((d, _Z_TILE), lambda bi, i, j: (0, j))],
        out_specs=pl.BlockSpec((1, tm, _Z_TILE), lambda bi, i, j: (bi, i, j)),
        out_shape=jax.ShapeDtypeStruct((b, s, _Z_COLS), BF16),
        scratch_shapes=[pltpu.VMEM((tm, d), BF16)],
        compiler_params=_params(("arbitrary", "arbitrary", "arbitrary")),
        name="inproj",
    )(x, mod, g_mix.reshape(1, d), w_z)


def _prep_kernel(cq_ref, ckv_ref, kr_ref, qg_ref, kg_ref, vg_ref,
                 gcq_ref, gckv_ref, gqn_ref, gkn_ref, wq_ref, wkv_ref,
                 c1_ref, s1_ref, c2_ref, s2_ref,
                 qm_ref, km_ref, vm_ref, qb_ref, kb_ref, vb_ref):
    c1, s1 = c1_ref[...], s1_ref[...]
    c2, s2 = c2_ref[...], s2_ref[...]
    ts = cq_ref.shape[1]
    one_col = (lax.broadcasted_iota(jnp.int32, (ts, LANES), 1) == 0).astype(BF16)

    cqn = (_rms(cq_ref[0].astype(F32)) * gcq_ref[...]).astype(BF16)
    q = jnp.dot(cqn, wq_ref[...], preferred_element_type=F32)
    q = q * ((MLA_NOPE_DIM + MLA_ROPE_DIM) ** -0.5)
    for h in range(MLA_HEADS):
        lo = h * MLA_QK_PAD
        qm_ref[0, :, lo:lo + LANES] = q[:, lo:lo + LANES].astype(BF16)
        qm_ref[0, :, lo + LANES:lo + 2 * LANES] = _rope128(
            q[:, lo + LANES:lo + 2 * LANES], c1, s1).astype(BF16)

    ckvn = (_rms(ckv_ref[0].astype(F32)) * gckv_ref[...]).astype(BF16)
    kv = jnp.dot(ckvn, wkv_ref[...], preferred_element_type=F32)
    krr = _rope128(kr_ref[0].astype(F32), c1, s1).astype(BF16)
    nk = MLA_HEADS * MLA_NOPE_DIM
    for h in range(MLA_HEADS):
        lo = h * MLA_QK_PAD
        km_ref[0, :, lo:lo + LANES] = kv[:, h * LANES:(h + 1) * LANES].astype(BF16)
        km_ref[0, :, lo + LANES:lo + 2 * LANES] = krr
    for h in range(MLA_HEADS):
        lo = 2 * h * MLA_V_DIM
        vm_ref[0, :, lo:lo + MLA_V_DIM] = kv[:, nk + h * MLA_V_DIM:nk + (h + 1) * MLA_V_DIM].astype(BF16)
        vm_ref[0, :, lo + MLA_V_DIM:lo + 2 * MLA_V_DIM] = one_col

    gqn = gqn_ref[...] * (GQA_HEAD_DIM ** -0.5)
    for h in range(GQA_HEADS):
        lo = h * GQA_HEAD_DIM
        t = _rms(qg_ref[0, :, lo:lo + GQA_HEAD_DIM].astype(F32)) * gqn
        qb_ref[0, :, lo:lo + GQA_HEAD_DIM] = _rope128(t, c2, s2).astype(BF16)
    for h in range(GQA_KV_HEADS):
        lo = h * GQA_HEAD_DIM
        t = _rms(kg_ref[0, :, lo:lo + GQA_HEAD_DIM].astype(F32)) * gkn_ref[...]
        kb_ref[0, :, lo:lo + GQA_HEAD_DIM] = _rope128(t, c2, s2).astype(BF16)
        vb_ref[0, :, 2 * lo:2 * lo + GQA_HEAD_DIM] = vg_ref[0, :, lo:lo + GQA_HEAD_DIM]
        vb_ref[0, :, 2 * lo + GQA_HEAD_DIM:2 * lo + 2 * GQA_HEAD_DIM] = one_col


def _prep(z, g_cq, g_ckv, g_qn, g_kn, wq, wkv, tabs, ts):
    b, s, _ = z.shape

    def zspec(width, col):
        return pl.BlockSpec((1, ts, width), lambda bi, i: (bi, i, col // width))

    def full(a):
        return pl.BlockSpec(a.shape, lambda bi, i: (0,) * a.ndim)

    tab_spec = pl.BlockSpec((ts, LANES), lambda bi, i: (i, 0))
    row = lambda g: g.reshape(1, -1)
    gs = [row(g_cq), row(g_ckv), row(g_qn), row(g_kn)]
    qk = MLA_HEADS * MLA_QK_PAD

    def ospec(width):
        return pl.BlockSpec((1, ts, width), lambda bi, i: (bi, i, 0))

    return pl.pallas_call(
        _prep_kernel,
        grid=(b, s // ts),
        in_specs=[zspec(MLA_Q_RANK, _Z_CQ), zspec(MLA_KV_RANK, _Z_CKV), zspec(LANES, _Z_KR),
                  zspec(GQA_HEADS * GQA_HEAD_DIM, _Z_QG), zspec(GQA_KV_HEADS * GQA_HEAD_DIM, _Z_KG),
                  zspec(GQA_KV_HEADS * GQA_HEAD_DIM, _Z_VG)]
                 + [full(g) for g in gs] + [full(wq), full(wkv)] + [tab_spec] * 4,
        out_specs=[ospec(qk), ospec(qk), ospec(2 * MLA_HEADS * MLA_V_DIM),
                   ospec(GQA_HEADS * GQA_HEAD_DIM), ospec(GQA_KV_HEADS * GQA_HEAD_DIM),
                   ospec(2 * GQA_KV_HEADS * GQA_HEAD_DIM)],
        out_shape=[jax.ShapeDtypeStruct((b, s, qk), BF16),
                   jax.ShapeDtypeStruct((b, s, qk), BF16),
                   jax.ShapeDtypeStruct((b, s, 2 * MLA_HEADS * MLA_V_DIM), BF16),
                   jax.ShapeDtypeStruct((b, s, GQA_HEADS * GQA_HEAD_DIM), BF16),
                   jax.ShapeDtypeStruct((b, s, GQA_KV_HEADS * GQA_HEAD_DIM), BF16),
                   jax.ShapeDtypeStruct((b, s, 2 * GQA_KV_HEADS * GQA_HEAD_DIM), BF16)],
        compiler_params=_params(("arbitrary", "arbitrary")),
        name="prep",
    )(z, z, z, z, z, z, *gs, wq, wkv, *tabs)


def _attn_kernel(q_ref, k_ref, v_ref, o_ref, *, heads, dk, dv, shared_kv):
    tk = k_ref.shape[1] // ATTN_KEY_CHUNKS
    for g in range(heads):
        kv = 0 if shared_kv else g
        q = q_ref[0, :, g * dk:(g + 1) * dk]
        m = acc = None
        for c in range(ATTN_KEY_CHUNKS):
            k = k_ref[0, c * tk:(c + 1) * tk, kv * dk:(kv + 1) * dk]
            v = v_ref[0, c * tk:(c + 1) * tk, 2 * kv * dv:2 * (kv + 1) * dv]
            s = lax.dot_general(q, k, (((1,), (1,)), ((), ())), preferred_element_type=F32)
            mc = jnp.max(s, axis=-1, keepdims=True)
            m_new = mc if m is None else jnp.maximum(m, mc)
            pv = jnp.dot(jnp.exp(s - m_new).astype(BF16), v, preferred_element_type=F32)
            acc = pv if acc is None else acc * jnp.exp(m - m_new) + pv
            m = m_new
        o_ref[0, :, g * dv:(g + 1) * dv] = (acc[:, :dv] / acc[:, dv:dv + 1]).astype(BF16)


def _attention(q, k, v, *, groups, heads, dk, dv, shared_kv, tq, name):
    b, s, _ = q.shape
    kvh = 1 if shared_kv else heads
    return pl.pallas_call(
        functools.partial(_attn_kernel, heads=heads, dk=dk, dv=dv, shared_kv=shared_kv),
        grid=(b, groups, s // tq),
        in_specs=[pl.BlockSpec((1, tq, heads * dk), lambda bi, g, i: (bi, i, g)),
                  pl.BlockSpec((1, s, kvh * dk), lambda bi, g, i: (bi, 0, g)),
                  pl.BlockSpec((1, s, kvh * 2 * dv), lambda bi, g, i: (bi, 0, g))],
        out_specs=pl.BlockSpec((1, tq, heads * dv), lambda bi, g, i: (bi, i, g)),
        out_shape=jax.ShapeDtypeStruct((b, s, groups * heads * dv), BF16),
        compiler_params=_params(("arbitrary", "arbitrary", "arbitrary")),
        name=name,
    )(q, k, v)


def _merge_kernel(oa_ref, ob_ref, ga_ref, gb_ref, x_ref, mod_ref, gffn_ref,
                  wa_ref, wb_ref, wo_ref, wr_ref,
                  x1_ref, h2_ref, lg_ref):
    a = jnp.dot(oa_ref[0], wa_ref[...], preferred_element_type=F32)
    bb = jnp.dot(ob_ref[0], wb_ref[...], preferred_element_type=F32)
    merged = (jax.nn.sigmoid(ga_ref[0].astype(F32)) * a
              + jax.nn.sigmoid(gb_ref[0].astype(F32)) * bb).astype(BF16)
    y = jnp.dot(merged, wo_ref[...], preferred_element_type=F32)
    x1 = x_ref[0] + mod_ref[0, 2:3, :] * y
    x1_ref[0] = x1
    h2 = _rms(x1) * gffn_ref[...] * (1.0 + mod_ref[0, 4:5, :]) + mod_ref[0, 3:4, :]
    hi = h2.astype(BF16)
    lo = (h2 - hi.astype(F32)).astype(BF16)
    h2_ref[0] = hi
    lg = jnp.dot(jnp.concatenate([hi, lo], axis=0), wr_ref[...], preferred_element_type=F32)
    tm = hi.shape[0]
    lg = lg[:tm] + lg[tm:]
    lg_ref[0] = lg[:, :LANES] + lg[:, LANES:]


def _merge(o_a, o_b, z, x, mod, g_ffn, wa, wb, wo, wr, tm):
    b, s, d = x.shape
    na, nb = o_a.shape[2], o_b.shape[2]
    tok = lambda width, col=0: pl.BlockSpec((1, tm, width), lambda bi, i: (bi, i, col // width))
    res = lambda a: _resident(a.shape, lambda bi, i: (0,) * a.ndim)
    return pl.pallas_call(
        _merge_kernel,
        grid=(b, s // tm),
        in_specs=[tok(na), tok(nb), tok(d, _Z_GATE_A), tok(d, _Z_GATE_B), tok(d),
                  pl.BlockSpec((1, 6, d), lambda bi, i: (bi, 0, 0)),
                  pl.BlockSpec((1, d), lambda bi, i: (0, 0)),
                  res(wa), res(wb), res(wo), res(wr)],
        out_specs=[tok(d), tok(d), tok(LANES)],
        out_shape=[jax.ShapeDtypeStruct((b, s, d), F32),
                   jax.ShapeDtypeStruct((b, s, d), BF16),
                   jax.ShapeDtypeStruct((b, s, LANES), F32)],
        compiler_params=_params(("arbitrary", "arbitrary")),
        name="merge",
    )(o_a, o_b, z, z, x, mod, g_ffn.reshape(1, d), wa, wb, wo, wr)


def _lane_cumsum(x):
    n = x.shape[1]
    lane = lax.broadcasted_iota(jnp.int32, x.shape, 1)
    d = 1
    while d < n:
        x = x + jnp.where(lane >= d, pltpu.roll(x, d, 1), 0.0)
        d *= 2
    return x


def _router_kernel(lg_ref, rank_ref, wval_ref, rankc_ref, *, cap):
    e = N_EXPERTS
    lt = jnp.transpose(lg_ref[0])[:e, :]
    m = jnp.max(lt, axis=0, keepdims=True)
    ex = jnp.exp(lt - m)
    aff = ex / jnp.sum(ex, axis=0, keepdims=True)

    def step(_, carry):
        lo, hi = carry
        mid = lo + ((hi - lo) >> 1)
        cnt = jnp.sum((aff >= pltpu.bitcast(mid, F32)).astype(jnp.int32), axis=1, keepdims=True)
        ok = cnt >= cap
        return jnp.where(ok, mid, lo), jnp.where(ok, hi, mid)

    lo0 = jnp.zeros((e, 1), jnp.int32)
    hi0 = jnp.full((e, 1), 0x3F800001, jnp.int32)
    thr_bits, _ = lax.fori_loop(0, 31, step, (lo0, hi0))
    thr = pltpu.bitcast(thr_bits, F32)

    gt = aff > thr
    eq = aff == thr
    need = (cap - jnp.sum(gt.astype(jnp.int32), axis=1, keepdims=True)).astype(F32)
    ceq = _lane_cumsum(eq.astype(F32))
    sel = gt | (eq & (ceq <= need))
    rank = jnp.where(sel, _lane_cumsum(sel.astype(F32)) - 1.0, -1.0)
    rank_ref[0] = rank
    wval_ref[0] = jnp.where(sel, aff, 0.0)
    pad = jnp.full((LANES - e, rank.shape[1]), -1.0, F32)
    rankc_ref[0] = jnp.transpose(jnp.concatenate([rank, pad], axis=0))


def _router(logits, cap):
    b, s, _ = logits.shape
    e = N_EXPERTS
    return pl.pallas_call(
        functools.partial(_router_kernel, cap=cap),
        grid=(b,),
        in_specs=[pl.BlockSpec((1, s, LANES), lambda bi: (bi, 0, 0))],
        out_specs=[pl.BlockSpec((1, e, s), lambda bi: (bi, 0, 0)),
                   pl.BlockSpec((1, e, s), lambda bi: (bi, 0, 0)),
                   pl.BlockSpec((1, s, LANES), lambda bi: (bi, 0, 0))],
        out_shape=[jax.ShapeDtypeStruct((b, e, s), F32),
                   jax.ShapeDtypeStruct((b, e, s), F32),
                   jax.ShapeDtypeStruct((b, s, LANES), F32)],
        compiler_params=_params(("arbitrary",)),
        name="router",
    )(logits)


def _gather_kernel(rank_ref, wval_ref, h2_ref, xg_ref, val_ref, *, cap):
    rank = rank_ref[0, 0]
    s = rank.shape[1]
    hot = rank == lax.broadcasted_iota(jnp.int32, (cap, s), 0).astype(F32)
    xg_ref[0, 0] = jnp.dot(hot.astype(BF16), h2_ref[0], preferred_element_type=F32).astype(BF16)
    val_ref[0, 0] = jnp.sum(jnp.where(hot, wval_ref[0, 0], 0.0), axis=1, keepdims=True)


def _gather(rank, wval, h2, cap):
    b, e, s = rank.shape
    d = h2.shape[2]
    r4 = rank.reshape(b, e, 1, s)
    w4 = wval.reshape(b, e, 1, s)
    row = pl.BlockSpec((1, 1, 1, s), lambda bi, ei: (bi, ei, 0, 0))
    return pl.pallas_call(
        functools.partial(_gather_kernel, cap=cap),
        grid=(b, e),
        in_specs=[row, row, pl.BlockSpec((1, s, d), lambda bi, ei: (bi, 0, 0))],
        out_specs=[pl.BlockSpec((1, 1, cap, d), lambda bi, ei: (bi, ei, 0, 0)),
                   pl.BlockSpec((1, 1, cap, 1), lambda bi, ei: (bi, ei, 0, 0))],
        out_shape=[jax.ShapeDtypeStruct((b, e, cap, d), BF16),
                   jax.ShapeDtypeStruct((b, e, cap, 1), F32)],
        compiler_params=_params(("arbitrary", "arbitrary")),
        name="gather",
    )(r4, w4, h2)


def _expert_kernel(xg_ref, val_ref, wg_ref, wu_ref, wd_ref, y_ref, wg_s, wu_s, wd_s, *, n_exp, n_b):
    p = pl.program_id(0)
    bi = pl.program_id(1)
    ng = n_b // 2
    rows = wd_ref.shape[1]
    stage = p % 2
    ready = 1 - stage

    @pl.when(p < n_exp)
    def _():
        @pl.when(bi < ng)
        def _():
            wg_s[stage, bi] = wg_ref[0].astype(BF16)

        @pl.when(bi >= ng)
        def _():
            wu_s[stage, bi - ng] = wu_ref[0].astype(BF16)

        wd_s[stage, pl.ds(pl.multiple_of(bi * rows, rows), rows), :] = wd_ref[0].astype(BF16)

    @pl.when(p > 0)
    def _():
        xg = xg_ref[0, 0]
        a = jnp.concatenate([jnp.dot(xg, wg_s[ready, c], preferred_element_type=F32)
                             for c in range(ng)], axis=1)
        u = jnp.concatenate([jnp.dot(xg, wu_s[ready, c], preferred_element_type=F32)
                             for c in range(ng)], axis=1)
        hm = (a * jax.nn.sigmoid(a) * u).astype(BF16)
        y = jnp.dot(hm, wd_s[ready], preferred_element_type=F32)
        y_ref[0, 0] = (y * val_ref[0, 0]).astype(BF16)


def _experts(xg, val, wg, wu, wd):
    b, e, cap, d = xg.shape
    f = wg.shape[2]
    assert b % 2 == 0 and f % (b // 2) == 0 and f % b == 0
    ng = b // 2
    fc, fr = f // ng, f // b
    last = e - 1
    tok = lambda p, bi: (jnp.where(p == 0, 0, bi), jnp.maximum(p - 1, 0), 0, 0)
    exp = lambda p: jnp.minimum(p, last)
    return pl.pallas_call(
        functools.partial(_expert_kernel, n_exp=e, n_b=b),
        grid=(e + 1, b),
        in_specs=[pl.BlockSpec((1, 1, cap, d), tok),
                  pl.BlockSpec((1, 1, cap, 1), tok),
                  pl.BlockSpec((1, d, fc), lambda p, bi: (
                      exp(p), 0, jnp.where(p <= last, jnp.minimum(bi, ng - 1), ng - 1))),
                  pl.BlockSpec((1, d, fc), lambda p, bi: (
                      exp(p), 0, jnp.where(p <= last, jnp.maximum(bi - ng, 0), ng - 1))),
                  pl.BlockSpec((1, fr, d), lambda p, bi: (
                      exp(p), jnp.where(p <= last, bi, b - 1), 0))],
        out_specs=pl.BlockSpec((1, 1, cap, d), tok),
        out_shape=jax.ShapeDtypeStruct((b, e, cap, d), BF16),
        scratch_shapes=[pltpu.VMEM((2, ng, d, fc), BF16),
                        pltpu.VMEM((2, ng, d, fc), BF16),
                        pltpu.VMEM((2, f, d), BF16)],
        compiler_params=_params(("arbitrary", "arbitrary")),
        name="experts",
    )(xg, val, wg, wu, wd)


def _combine_kernel(rankc_ref, y_ref, x1_ref, mod_ref, gfin_ref, o_ref, *, cap, final):
    tm = rankc_ref.shape[1]
    rc = rankc_ref[0]
    col = lax.broadcasted_iota(jnp.int32, (tm, cap), 1).astype(F32)
    hot = jnp.concatenate(
        [(rc[:, e:e + 1] == col).astype(BF16) for e in range(N_EXPERTS)], axis=1)
    moe = jnp.dot(hot, y_ref[0], preferred_element_type=F32)
    x2 = x1_ref[0] + mod_ref[0, 5:6, :] * moe
    if final:
        x2 = _rms(x2) * gfin_ref[...]
    o_ref[0] = x2


def _combine(rankc, y, x1, mod, g_final, cap, tm, final):
    b, s, d = x1.shape
    y2 = y.reshape(b, N_EXPERTS * cap, d)
    tok = lambda width: pl.BlockSpec((1, tm, width), lambda bi, i: (bi, i, 0))
    return pl.pallas_call(
        functools.partial(_combine_kernel, cap=cap, final=final),
        grid=(b, s // tm),
        in_specs=[tok(LANES),
                  pl.BlockSpec((1, N_EXPERTS * cap, d), lambda bi, i: (bi, 0, 0)),
                  tok(d),
                  pl.BlockSpec((1, 6, d), lambda bi, i: (bi, 0, 0)),
                  pl.BlockSpec((1, d), lambda bi, i: (0, 0))],
        out_specs=tok(d),
        out_shape=jax.ShapeDtypeStruct((b, s, d), F32),
        compiler_params=_params(("arbitrary", "arbitrary")),
        name="combine",
    )(rankc, y2, x1, mod, g_final.reshape(1, d))


def _rope_tables(s):
    half = MLA_ROPE_DIM // 2
    freqs = ROPE_THETA ** (-jnp.arange(half, dtype=F32) / half)
    t = jnp.arange(s)
    ang = lambda pos: pos.astype(F32)[:, None] * freqs[None, :]
    a1, ar, ac = ang(t), ang(t // GRID_W), ang(t % GRID_W)
    z = jnp.zeros((s, half), F32)
    cat = lambda *p: jnp.concatenate(p, axis=1)
    c1 = cat(jnp.cos(a1), z, jnp.cos(a1), z)
    s1 = cat(-jnp.sin(a1), z, jnp.sin(a1), z)
    c2 = cat(jnp.cos(ar), jnp.cos(ac), jnp.cos(ar), jnp.cos(ac))
    s2 = cat(-jnp.sin(ar), -jnp.sin(ac), jnp.sin(ar), jnp.sin(ac))
    return c1, s1, c2, s2


def _pair_apart(w, heads):
    lead = w.shape[:-1]
    w = w.reshape(*lead, heads, 4, GQA_HEAD_DIM // 4)
    return w[..., jnp.array([0, 2, 1, 3]), :].reshape(*lead, heads * GQA_HEAD_DIM)


def _spread_rope(w):
    half = MLA_ROPE_DIM // 2
    z = jnp.zeros(w.shape[:-1] + (half,), w.dtype)
    return jnp.concatenate([w[..., :half], z, w[..., half:], z], axis=-1)


def _regroup_w_in(w_in):
    d = w_in.shape[0]
    w_in = w_in.astype(BF16)
    widths = (MLA_Q_RANK, MLA_KV_RANK, MLA_ROPE_DIM, GQA_HEADS * GQA_HEAD_DIM,
              GQA_KV_HEADS * GQA_HEAD_DIM, GQA_KV_HEADS * GQA_HEAD_DIM, d, d)
    offs = [0]
    for w in widths:
        offs.append(offs[-1] + w)
    cq, ckv, kr, qg, kg, vg, ga, gb = [w_in[:, offs[i]:offs[i + 1]] for i in range(8)]
    parts = [ga, gb, _pair_apart(qg, GQA_HEADS), _pair_apart(kg, GQA_KV_HEADS), vg, cq, ckv,
             _spread_rope(kr)]
    used = sum(p.shape[1] for p in parts)
    parts.append(jnp.zeros((d, _Z_COLS - used), BF16))
    return jnp.concatenate(parts, axis=1)


def _regroup_w_uq(w_uq):
    r = w_uq.shape[0]
    w = w_uq.astype(BF16).reshape(r, MLA_HEADS, MLA_NOPE_DIM + MLA_ROPE_DIM)
    w = jnp.concatenate([w[..., :MLA_NOPE_DIM], _spread_rope(w[..., MLA_NOPE_DIM:])], axis=-1)
    return w.reshape(r, MLA_HEADS * MLA_QK_PAD)


def _regroup_w_ukv(w_ukv):
    r = w_ukv.shape[0]
    w = w_ukv.reshape(r, MLA_HEADS, MLA_NOPE_DIM + MLA_V_DIM)
    wk = w[:, :, :MLA_NOPE_DIM].reshape(r, MLA_HEADS * MLA_NOPE_DIM)
    wv = w[:, :, MLA_NOPE_DIM:].reshape(r, MLA_HEADS * MLA_V_DIM)
    return jnp.concatenate([wk, wv], axis=1).astype(BF16)


def _split_router(w_router):
    d, e = w_router.shape
    w = jnp.pad(w_router, ((0, 0), (0, LANES - e)))
    hi = w.astype(BF16)
    lo = (w - hi.astype(F32)).astype(BF16)
    return jnp.concatenate([hi, lo], axis=1)


def kernel(x, c, w_ada, b_ada, g_mix, w_in, g_cq, g_ckv, w_uq, w_ukv, g_qn, g_kn, w_br_a, w_br_b, w_out, g_ffn, w_router, w_e_gate, w_e_up, w_e_down, g_final):
    b, s, d = x.shape
    depth = w_ada.shape[0]
    cap = EC_CAPACITY * s // N_EXPERTS
    assert d == _Z_GATE_B and s % LANES == 0 and cap % LANES == 0
    tm_in = min(1024, s)
    tq = min(256, s)
    tabs = _rope_tables(s)
    for l in range(depth):
        mod = _adaln(c, w_ada[l], b_ada[l]).reshape(b, 6, d)
        z = _inproj(x, mod, g_mix[l], _regroup_w_in(w_in[l]), tm_in)
        qm, km, vm, qb, kb, vb = _prep(z, g_cq[l], g_ckv[l], _pair_apart(g_qn[l], 1),
                                       _pair_apart(g_kn[l], 1), _regroup_w_uq(w_uq[l]),
                                       _regroup_w_ukv(w_ukv[l]), tabs, tq)
        o_a = _attention(qm, km, vm, groups=MLA_HEADS // MLA_HEADS_PER_STEP,
                         heads=MLA_HEADS_PER_STEP, dk=MLA_QK_PAD, dv=MLA_V_DIM, shared_kv=False,
                         tq=tq, name="mla_attn")
        o_b = _attention(qb, kb, vb, groups=GQA_KV_HEADS, heads=GQA_GROUP, dk=GQA_HEAD_DIM,
                         dv=GQA_HEAD_DIM, shared_kv=True, tq=tq, name="gqa_attn")
        wr = _split_router(w_router[l])
        x1, h2, logits = _merge(o_a, o_b, z, x, mod, g_ffn[l], w_br_a[l].astype(BF16),
                                w_br_b[l].astype(BF16), w_out[l].astype(BF16), wr, tq)
        rank, wval, rankc = _router(logits, cap)
        xg, val = _gather(rank, wval, h2, cap)
        y = _experts(xg, val, w_e_gate[l], w_e_up[l], w_e_down[l])
        x = _combine(rankc, y, x1, mod, g_final, cap, tq, final=(l == depth - 1))
    return x
```

```python
import functools

import jax
import jax.numpy as jnp
from jax import lax
from jax.experimental import pallas as pl
from jax.experimental.pallas import tpu as pltpu

F32 = jnp.float32
BF16 = jnp.bfloat16

GRID_W = 64
ROPE_THETA = 10000.0
RMS_EPS = 1e-6

MLA_HEADS = 8
MLA_Q_RANK = 512
MLA_KV_RANK = 256
MLA_NOPE_DIM = 128
MLA_ROPE_DIM = 64
MLA_V_DIM = 128
MLA_QK_PAD = 256
MLA_HEADS_PER_STEP = 4
ATTN_KEY_CHUNKS = 4

GQA_HEADS = 8
GQA_KV_HEADS = 2
GQA_HEAD_DIM = 128
GQA_GROUP = GQA_HEADS // GQA_KV_HEADS

N_EXPERTS = 16
EC_CAPACITY = 2

LANES = 128

INPROJ_ROWS = 1024
NORM_CHUNK = 128
PREP_ROWS = 512
ATTN_Q_ROWS = 512
MERGE_ROWS = 256
COMBINE_ROWS = 256
V7X_VMEM_LIMIT = 56 * 1024 * 1024

_Z_GATE_A = 0
_Z_GATE_B = 2048
_Z_QG = 4096
_Z_KG = 5120
_Z_VG = 5376
_Z_CQ = 5632
_Z_CKV = 6144
_Z_KR = 6400
_Z_COLS = 6656
_Z_TILE = 512


def _params(sem, vmem=V7X_VMEM_LIMIT):
    return pltpu.CompilerParams(dimension_semantics=sem, vmem_limit_bytes=vmem)


def _resident(shape, index_map):
    return pl.BlockSpec(shape, index_map, pipeline_mode=pl.Buffered(1))


def _rms(x):
    return x * lax.rsqrt(jnp.mean(x * x, axis=-1, keepdims=True) + RMS_EPS)


def _rope128(x, cos, sin):
    return x * cos + pltpu.roll(x, LANES // 2, 1) * sin


def _adaln_kernel(c_ref, w_ref, b_ref, o_ref):
    c = c_ref[...]
    sc = (c * jax.nn.sigmoid(c)).astype(BF16)
    o_ref[...] = jnp.dot(sc, w_ref[...].astype(BF16), preferred_element_type=F32) + b_ref[...]


def _adaln(c, w_ada, b_ada):
    b, d = c.shape
    n = w_ada.shape[1]
    tn = 1024
    return pl.pallas_call(
        _adaln_kernel,
        grid=(n // tn,),
        in_specs=[pl.BlockSpec((b, d), lambda j: (0, 0)),
                  pl.BlockSpec((d, tn), lambda j: (0, j)),
                  pl.BlockSpec((1, tn), lambda j: (0, j))],
        out_specs=pl.BlockSpec((b, tn), lambda j: (0, j)),
        out_shape=jax.ShapeDtypeStruct((b, n), F32),
        compiler_params=_params(("arbitrary",)),
        name="adaln",
    )(c, w_ada, b_ada.reshape(1, n))


def _inproj_kernel(x_ref, mod_ref, g_ref, wa_ref, wb_ref, ws_ref, z_ref, h_scr):
    j = pl.program_id(2)

    @pl.when(j == 0)
    def _():
        gain = g_ref[...] * (1.0 + mod_ref[0, 1:2, :])
        shift = mod_ref[0, 0:1, :]

        def body(i, carry):
            r = pl.ds(pl.multiple_of(i * NORM_CHUNK, NORM_CHUNK), NORM_CHUNK)
            h_scr[r, :] = (_rms(x_ref[0, r, :]) * gain + shift).astype(BF16)
            return carry

        lax.fori_loop(0, h_scr.shape[0] // NORM_CHUNK, body, 0)

    def project(w_ref):
        z_ref[0] = jnp.dot(h_scr[...], w_ref[...], preferred_element_type=F32).astype(BF16)

    na = _Z_GATE_B // _Z_TILE
    pl.when(j < na)(lambda: project(wa_ref))
    pl.when((j >= na) & (j < 2 * na))(lambda: project(wb_ref))
    pl.when(j >= 2 * na)(lambda: project(ws_ref))


def _inproj(x, mod, g_mix, w_z, tm):
    b, s, d = x.shape
    nj = _Z_COLS // _Z_TILE
    wa, wb, ws = w_z
    na = _Z_GATE_B // _Z_TILE
    return pl.pallas_call(
        _inproj_kernel,
        grid=(b, s // tm, nj),
        in_specs=[pl.BlockSpec((1, tm, d), lambda bi, i, j: (bi, i, 0)),
                  pl.BlockSpec((1, 6, d), lambda bi, i, j: (bi, 0, 0)),
                  pl.BlockSpec((1, d), lambda bi, i, j: (0, 0)),
                  pl.BlockSpec((d, _Z_TILE), lambda bi, i, j: (0, jnp.clip(j, 0, na - 1))),
                  pl.BlockSpec((d, _Z_TILE), lambda bi, i, j: (0, jnp.clip(j - na, 0, na - 1))),
                  pl.BlockSpec((d, _Z_TILE),
                               lambda bi, i, j: (0, jnp.clip(j - 2 * na, 0, nj - 2 * na - 1)))],
        out_specs=pl.BlockSpec((1, tm, _Z_TILE), lambda bi, i, j: (bi, i, j)),
        out_shape=jax.ShapeDtypeStruct((b, s, _Z_COLS), BF16),
        scratch_shapes=[pltpu.VMEM((tm, d), BF16)],
        compiler_params=_params(("arbitrary", "arbitrary", "arbitrary")),
        name="inproj",
    )(x, mod, g_mix.reshape(1, d), wa, wb, ws)


def _prep_kernel(cq_ref, ckv_ref, kr_ref, qg_ref, kg_ref, vg_ref,
                 gcq_ref, gckv_ref, gqn_ref, gkn_ref, wq_ref, wkv_ref,
                 c1_ref, s1_ref, c2_ref, s2_ref,
                 qm_ref, km_ref, vm_ref, qb_ref, kb_ref, vb_ref):
    c1, s1 = c1_ref[...], s1_ref[...]
    c2, s2 = c2_ref[...], s2_ref[...]
    ts = cq_ref.shape[1]
    one_col = (lax.broadcasted_iota(jnp.int32, (ts, LANES), 1) == 0).astype(BF16)

    cqn = (_rms(cq_ref[0].astype(F32)) * gcq_ref[...]).astype(BF16)
    q = jnp.dot(cqn, wq_ref[...], preferred_element_type=F32)
    q = q * ((MLA_NOPE_DIM + MLA_ROPE_DIM) ** -0.5)
    for h in range(MLA_HEADS):
        lo = h * MLA_QK_PAD
        qm_ref[0, :, lo:lo + LANES] = q[:, lo:lo + LANES].astype(BF16)
        qm_ref[0, :, lo + LANES:lo + 2 * LANES] = _rope128(
            q[:, lo + LANES:lo + 2 * LANES], c1, s1).astype(BF16)

    ckvn = (_rms(ckv_ref[0].astype(F32)) * gckv_ref[...]).astype(BF16)
    kv = jnp.dot(ckvn, wkv_ref[...], preferred_element_type=F32)
    krr = _rope128(kr_ref[0].astype(F32), c1, s1).astype(BF16)
    nk = MLA_HEADS * MLA_NOPE_DIM
    for h in range(MLA_HEADS):
        lo = h * MLA_QK_PAD
        km_ref[0, :, lo:lo + LANES] = kv[:, h * LANES:(h + 1) * LANES].astype(BF16)
        km_ref[0, :, lo + LANES:lo + 2 * LANES] = krr
    for h in range(MLA_HEADS):
        lo = 2 * h * MLA_V_DIM
        vm_ref[0, :, lo:lo + MLA_V_DIM] = kv[:, nk + h * MLA_V_DIM:nk + (h + 1) * MLA_V_DIM].astype(BF16)
        vm_ref[0, :, lo + MLA_V_DIM:lo + 2 * MLA_V_DIM] = one_col

    gqn = gqn_ref[...] * (GQA_HEAD_DIM ** -0.5)
    for h in range(GQA_HEADS):
        lo = h * GQA_HEAD_DIM
        t = _rms(qg_ref[0, :, lo:lo + GQA_HEAD_DIM].astype(F32)) * gqn
        qb_ref[0, :, lo:lo + GQA_HEAD_DIM] = _rope128(t, c2, s2).astype(BF16)
    for h in range(GQA_KV_HEADS):
        lo = h * GQA_HEAD_DIM
        t = _rms(kg_ref[0, :, lo:lo + GQA_HEAD_DIM].astype(F32)) * gkn_ref[...]
        kb_ref[0, :, lo:lo + GQA_HEAD_DIM] = _rope128(t, c2, s2).astype(BF16)
        vb_ref[0, :, 2 * lo:2 * lo + GQA_HEAD_DIM] = vg_ref[0, :, lo:lo + GQA_HEAD_DIM]
        vb_ref[0, :, 2 * lo + GQA_HEAD_DIM:2 * lo + 2 * GQA_HEAD_DIM] = one_col


def _prep(z, g_cq, g_ckv, g_qn, g_kn, wq, wkv, tabs, ts):
    b, s, _ = z.shape

    def zspec(width, col):
        return pl.BlockSpec((1, ts, width), lambda bi, i: (bi, i, col // width))

    def full(a):
        return pl.BlockSpec(a.shape, lambda bi, i: (0,) * a.ndim)

    tab_spec = pl.BlockSpec((ts, LANES), lambda bi, i: (i, 0))
    row = lambda g: g.reshape(1, -1)
    gs = [row(g_cq), row(g_ckv), row(g_qn), row(g_kn)]
    qk = MLA_HEADS * MLA_QK_PAD

    def ospec(width):
        return pl.BlockSpec((1, ts, width), lambda bi, i: (bi, i, 0))

    return pl.pallas_call(
        _prep_kernel,
        grid=(b, s // ts),
        in_specs=[zspec(MLA_Q_RANK, _Z_CQ), zspec(MLA_KV_RANK, _Z_CKV), zspec(LANES, _Z_KR),
                  zspec(GQA_HEADS * GQA_HEAD_DIM, _Z_QG), zspec(GQA_KV_HEADS * GQA_HEAD_DIM, _Z_KG),
                  zspec(GQA_KV_HEADS * GQA_HEAD_DIM, _Z_VG)]
                 + [full(g) for g in gs] + [full(wq), full(wkv)] + [tab_spec] * 4,
        out_specs=[ospec(qk), ospec(qk), ospec(2 * MLA_HEADS * MLA_V_DIM),
                   ospec(GQA_HEADS * GQA_HEAD_DIM), ospec(GQA_KV_HEADS * GQA_HEAD_DIM),
                   ospec(2 * GQA_KV_HEADS * GQA_HEAD_DIM)],
        out_shape=[jax.ShapeDtypeStruct((b, s, qk), BF16),
                   jax.ShapeDtypeStruct((b, s, qk), BF16),
                   jax.ShapeDtypeStruct((b, s, 2 * MLA_HEADS * MLA_V_DIM), BF16),
                   jax.ShapeDtypeStruct((b, s, GQA_HEADS * GQA_HEAD_DIM), BF16),
                   jax.ShapeDtypeStruct((b, s, GQA_KV_HEADS * GQA_HEAD_DIM), BF16),
                   jax.ShapeDtypeStruct((b, s, 2 * GQA_KV_HEADS * GQA_HEAD_DIM), BF16)],
        compiler_params=_params(("arbitrary", "arbitrary")),
        name="prep",
    )(z, z, z, z, z, z, *gs, wq, wkv, *tabs)


def _attn_kernel(q_ref, k_ref, v_ref, o_ref, *, heads, dk, dv, shared_kv):
    tk = k_ref.shape[1] // ATTN_KEY_CHUNKS
    for g in range(heads):
        kv = 0 if shared_kv else g
        q = q_ref[0, :, g * dk:(g + 1) * dk]
        m = acc = None
        for c in range(ATTN_KEY_CHUNKS):
            k = k_ref[0, c * tk:(c + 1) * tk, kv * dk:(kv + 1) * dk]
            v = v_ref[0, c * tk:(c + 1) * tk, 2 * kv * dv:2 * (kv + 1) * dv]
            s = lax.dot_general(q, k, (((1,), (1,)), ((), ())), preferred_element_type=F32)
            mc = jnp.max(s, axis=-1, keepdims=True)
            m_new = mc if m is None else jnp.maximum(m, mc)
            pv = jnp.dot(jnp.exp(s - m_new).astype(BF16), v, preferred_element_type=F32)
            acc = pv if acc is None else acc * jnp.exp(m - m_new) + pv
            m = m_new
        o_ref[0, :, g * dv:(g + 1) * dv] = (acc[:, :dv] / acc[:, dv:dv + 1]).astype(BF16)


def _attention(q, k, v, *, groups, heads, dk, dv, shared_kv, tq, name):
    b, s, _ = q.shape
    kvh = 1 if shared_kv else heads
    return pl.pallas_call(
        functools.partial(_attn_kernel, heads=heads, dk=dk, dv=dv, shared_kv=shared_kv),
        grid=(b, groups, s // tq),
        in_specs=[pl.BlockSpec((1, tq, heads * dk), lambda bi, g, i: (bi, i, g)),
                  pl.BlockSpec((1, s, kvh * dk), lambda bi, g, i: (bi, 0, g)),
                  pl.BlockSpec((1, s, kvh * 2 * dv), lambda bi, g, i: (bi, 0, g))],
        out_specs=pl.BlockSpec((1, tq, heads * dv), lambda bi, g, i: (bi, i, g)),
        out_shape=jax.ShapeDtypeStruct((b, s, groups * heads * dv), BF16),
        compiler_params=_params(("arbitrary", "arbitrary", "arbitrary")),
        name=name,
    )(q, k, v)


def _merge_kernel(oa_ref, ob_ref, ga_ref, gb_ref, x_ref, mod_ref, gffn_ref,
                  wa_ref, wb_ref, wo_ref, wr_ref,
                  x1_ref, h2_ref, lg_ref):
    a = jnp.dot(oa_ref[0], wa_ref[...], preferred_element_type=F32)
    bb = jnp.dot(ob_ref[0], wb_ref[...], preferred_element_type=F32)
    merged = (jax.nn.sigmoid(ga_ref[0].astype(F32)) * a
              + jax.nn.sigmoid(gb_ref[0].astype(F32)) * bb).astype(BF16)
    y = jnp.dot(merged, wo_ref[...], preferred_element_type=F32)
    x1 = x_ref[0] + mod_ref[0, 2:3, :] * y
    x1_ref[0] = x1
    h2 = _rms(x1) * gffn_ref[...] * (1.0 + mod_ref[0, 4:5, :]) + mod_ref[0, 3:4, :]
    hi = h2.astype(BF16)
    lo = (h2 - hi.astype(F32)).astype(BF16)
    h2_ref[0] = hi
    lg = jnp.dot(jnp.concatenate([hi, lo], axis=0), wr_ref[...], preferred_element_type=F32)
    tm = hi.shape[0]
    lg = lg[:tm] + lg[tm:]
    lg_ref[0] = lg[:, :LANES] + lg[:, LANES:]


def _merge(o_a, o_b, z, x, mod, g_ffn, wa, wb, wo, wr, tm):
    b, s, d = x.shape
    na, nb = o_a.shape[2], o_b.shape[2]
    tok = lambda width, col=0: pl.BlockSpec((1, tm, width), lambda bi, i: (bi, i, col // width))
    res = lambda a: _resident(a.shape, lambda bi, i: (0,) * a.ndim)
    return pl.pallas_call(
        _merge_kernel,
        grid=(b, s // tm),
        in_specs=[tok(na), tok(nb), tok(d, _Z_GATE_A), tok(d, _Z_GATE_B), tok(d),
                  pl.BlockSpec((1, 6, d), lambda bi, i: (bi, 0, 0)),
                  pl.BlockSpec((1, d), lambda bi, i: (0, 0)),
                  res(wa), res(wb), res(wo), res(wr)],
        out_specs=[tok(d), tok(d), tok(LANES)],
        out_shape=[jax.ShapeDtypeStruct((b, s, d), F32),
                   jax.ShapeDtypeStruct((b, s, d), BF16),
                   jax.ShapeDtypeStruct((b, s, LANES), F32)],
        compiler_params=_params(("arbitrary", "arbitrary")),
        name="merge",
    )(o_a, o_b, z, z, x, mod, g_ffn.reshape(1, d), wa, wb, wo, wr)


def _lane_cumsum(x):
    n = x.shape[1]
    lane = lax.broadcasted_iota(jnp.int32, x.shape, 1)
    d = 1
    while d < n:
        x = x + jnp.where(lane >= d, pltpu.roll(x, d, 1), 0.0)
        d *= 2
    return x


def _router_kernel(lg_ref, rank_ref, wval_ref, rankc_ref, *, cap):
    e = N_EXPERTS
    lt = jnp.transpose(lg_ref[0])[:e, :]
    m = jnp.max(lt, axis=0, keepdims=True)
    ex = jnp.exp(lt - m)
    aff = ex / jnp.sum(ex, axis=0, keepdims=True)

    def step(_, carry):
        lo, hi = carry
        mid = lo + ((hi - lo) >> 1)
        cnt = jnp.sum((aff >= pltpu.bitcast(mid, F32)).astype(jnp.int32), axis=1, keepdims=True)
        ok = cnt >= cap
        return jnp.where(ok, mid, lo), jnp.where(ok, hi, mid)

    lo0 = jnp.zeros((e, 1), jnp.int32)
    hi0 = jnp.full((e, 1), 0x3F800001, jnp.int32)
    thr_bits, _ = lax.fori_loop(0, 31, step, (lo0, hi0))
    thr = pltpu.bitcast(thr_bits, F32)

    gt = aff > thr
    eq = aff == thr
    need = (cap - jnp.sum(gt.astype(jnp.int32), axis=1, keepdims=True)).astype(F32)
    ceq = _lane_cumsum(eq.astype(F32))
    sel = gt | (eq & (ceq <= need))
    rank = jnp.where(sel, _lane_cumsum(sel.astype(F32)) - 1.0, -1.0)
    rank_ref[0] = rank
    wval_ref[0] = jnp.where(sel, aff, 0.0)
    pad = jnp.full((LANES - e, rank.shape[1]), -1.0, F32)
    rankc_ref[0] = jnp.transpose(jnp.concatenate([rank, pad], axis=0))


def _router(logits, cap):
    b, s, _ = logits.shape
    e = N_EXPERTS
    return pl.pallas_call(
        functools.partial(_router_kernel, cap=cap),
        grid=(b,),
        in_specs=[pl.BlockSpec((1, s, LANES), lambda bi: (bi, 0, 0))],
        out_specs=[pl.BlockSpec((1, e, s), lambda bi: (bi, 0, 0)),
                   pl.BlockSpec((1, e, s), lambda bi: (bi, 0, 0)),
                   pl.BlockSpec((1, s, LANES), lambda bi: (bi, 0, 0))],
        out_shape=[jax.ShapeDtypeStruct((b, e, s), F32),
                   jax.ShapeDtypeStruct((b, e, s), F32),
                   jax.ShapeDtypeStruct((b, s, LANES), F32)],
        compiler_params=_params(("arbitrary",)),
        name="router",
    )(logits)


def _gather_kernel(rank_ref, wval_ref, h2_ref, xg_ref, val_ref, *, cap):
    rank = rank_ref[0, 0]
    s = rank.shape[1]
    hot = rank == lax.broadcasted_iota(jnp.int32, (cap, s), 0).astype(F32)
    xg_ref[0, 0] = jnp.dot(hot.astype(BF16), h2_ref[0], preferred_element_type=F32).astype(BF16)
    val_ref[0, 0] = jnp.sum(jnp.where(hot, wval_ref[0, 0], 0.0), axis=1, keepdims=True)


def _gather(rank, wval, h2, cap):
    b, e, s = rank.shape
    d = h2.shape[2]
    r4 = rank.reshape(b, e, 1, s)
    w4 = wval.reshape(b, e, 1, s)
    row = pl.BlockSpec((1, 1, 1, s), lambda bi, ei: (bi, ei, 0, 0))
    return pl.pallas_call(
        functools.partial(_gather_kernel, cap=cap),
        grid=(b, e),
        in_specs=[row, row, pl.BlockSpec((1, s, d), lambda bi, ei: (bi, 0, 0))],
        out_specs=[pl.BlockSpec((1, 1, cap, d), lambda bi, ei: (bi, ei, 0, 0)),
                   pl.BlockSpec((1, 1, cap, 1), lambda bi, ei: (bi, ei, 0, 0))],
        out_shape=[jax.ShapeDtypeStruct((b, e, cap, d), BF16),
                   jax.ShapeDtypeStruct((b, e, cap, 1), F32)],
        compiler_params=_params(("arbitrary", "arbitrary")),
        name="gather",
    )(r4, w4, h2)


def _expert_kernel(xg_ref, val_ref, wg_ref, wu_ref, wd_ref, y_ref, wg_s, wu_s, wd_s, *, n_exp, n_b):
    p = pl.program_id(0)
    bi = pl.program_id(1)
    ng = n_b // 2
    rows = wd_ref.shape[1]
    stage = p % 2
    ready = 1 - stage

    @pl.when(p < n_exp)
    def _():
        @pl.when(bi < ng)
        def _():
            wg_s[stage, bi] = wg_ref[0].astype(BF16)

        @pl.when(bi >= ng)
        def _():
            wu_s[stage, bi - ng] = wu_ref[0].astype(BF16)

        wd_s[stage, pl.ds(pl.multiple_of(bi * rows, rows), rows), :] = wd_ref[0].astype(BF16)

    @pl.when(p > 0)
    def _():
        xg = xg_ref[0, 0]
        a = jnp.concatenate([jnp.dot(xg, wg_s[ready, c], preferred_element_type=F32)
                             for c in range(ng)], axis=1)
        u = jnp.concatenate([jnp.dot(xg, wu_s[ready, c], preferred_element_type=F32)
                             for c in range(ng)], axis=1)
        hm = (a * jax.nn.sigmoid(a) * u).astype(BF16)
        y = jnp.dot(hm, wd_s[ready], preferred_element_type=F32)
        y_ref[0, 0] = (y * val_ref[0, 0]).astype(BF16)


def _experts(xg, val, wg, wu, wd):
    b, e, cap, d = xg.shape
    f = wg.shape[2]
    assert b % 2 == 0 and f % (b // 2) == 0 and f % b == 0
    ng = b // 2
    fc, fr = f // ng, f // b
    last = e - 1
    tok = lambda p, bi: (jnp.where(p == 0, 0, bi), jnp.maximum(p - 1, 0), 0, 0)
    exp = lambda p: jnp.minimum(p, last)
    return pl.pallas_call(
        functools.partial(_expert_kernel, n_exp=e, n_b=b),
        grid=(e + 1, b),
        in_specs=[pl.BlockSpec((1, 1, cap, d), tok),
                  pl.BlockSpec((1, 1, cap, 1), tok),
                  pl.BlockSpec((1, d, fc), lambda p, bi: (
                      exp(p), 0, jnp.where(p <= last, jnp.minimum(bi, ng - 1), ng - 1))),
                  pl.BlockSpec((1, d, fc), lambda p, bi: (
                      exp(p), 0, jnp.where(p <= last, jnp.maximum(bi - ng, 0), ng - 1))),
                  pl.BlockSpec((1, fr, d), lambda p, bi: (
                      exp(p), jnp.where(p <= last, bi, b - 1), 0))],
        out_specs=pl.BlockSpec((1, 1, cap, d), tok),
        out_shape=jax.ShapeDtypeStruct((b, e, cap, d), BF16),
        scratch_shapes=[pltpu.VMEM((2, ng, d, fc), BF16),
                        pltpu.VMEM((2, ng, d, fc), BF16),
                        pltpu.VMEM((2, f, d), BF16)],
        compiler_params=_params(("arbitrary", "arbitrary")),
        name="experts",
    )(xg, val, wg, wu, wd)


def _combine_kernel(rankc_ref, y_ref, x1_ref, mod_ref, gfin_ref, o_ref, *, cap, final):
    tm = rankc_ref.shape[1]
    rc = rankc_ref[0]
    col = lax.broadcasted_iota(jnp.int32, (tm, cap), 1).astype(F32)
    hot = jnp.concatenate(
        [(rc[:, e:e + 1] == col).astype(BF16) for e in range(N_EXPERTS)], axis=1)
    moe = jnp.dot(hot, y_ref[0], preferred_element_type=F32)
    x2 = x1_ref[0] + mod_ref[0, 5:6, :] * moe
    if final:
        x2 = _rms(x2) * gfin_ref[...]
    o_ref[0] = x2


def _combine(rankc, y, x1, mod, g_final, cap, tm, final):
    b, s, d = x1.shape
    y2 = y.reshape(b, N_EXPERTS * cap, d)
    tok = lambda width: pl.BlockSpec((1, tm, width), lambda bi, i: (bi, i, 0))
    return pl.pallas_call(
        functools.partial(_combine_kernel, cap=cap, final=final),
        grid=(b, s // tm),
        in_specs=[tok(LANES),
                  pl.BlockSpec((1, N_EXPERTS * cap, d), lambda bi, i: (bi, 0, 0)),
                  tok(d),
                  pl.BlockSpec((1, 6, d), lambda bi, i: (bi, 0, 0)),
                  pl.BlockSpec((1, d), lambda bi, i: (0, 0))],
        out_specs=tok(d),
        out_shape=jax.ShapeDtypeStruct((b, s, d), F32),
        compiler_params=_params(("arbitrary", "arbitrary")),
        name="combine",
    )(rankc, y2, x1, mod, g_final.reshape(1, d))


def _rope_tables(s):
    half = MLA_ROPE_DIM // 2
    freqs = ROPE_THETA ** (-jnp.arange(half, dtype=F32) / half)
    t = jnp.arange(s)
    ang = lambda pos: pos.astype(F32)[:, None] * freqs[None, :]
    a1, ar, ac = ang(t), ang(t // GRID_W), ang(t % GRID_W)
    z = jnp.zeros((s, half), F32)
    cat = lambda *p: jnp.concatenate(p, axis=1)
    c1 = cat(jnp.cos(a1), z, jnp.cos(a1), z)
    s1 = cat(-jnp.sin(a1), z, jnp.sin(a1), z)
    c2 = cat(jnp.cos(ar), jnp.cos(ac), jnp.cos(ar), jnp.cos(ac))
    s2 = cat(-jnp.sin(ar), -jnp.sin(ac), jnp.sin(ar), jnp.sin(ac))
    return c1, s1, c2, s2


def _pair_apart(w, heads):
    lead = w.shape[:-1]
    w = w.reshape(*lead, heads, 4, GQA_HEAD_DIM // 4)
    return w[..., jnp.array([0, 2, 1, 3]), :].reshape(*lead, heads * GQA_HEAD_DIM)


def _spread_rope(w):
    half = MLA_ROPE_DIM // 2
    z = jnp.zeros(w.shape[:-1] + (half,), w.dtype)
    return jnp.concatenate([w[..., :half], z, w[..., half:], z], axis=-1)


def _regroup_w_in(w_in):
    d = w_in.shape[0]
    widths = (MLA_Q_RANK, MLA_KV_RANK, MLA_ROPE_DIM, GQA_HEADS * GQA_HEAD_DIM,
              GQA_KV_HEADS * GQA_HEAD_DIM, GQA_KV_HEADS * GQA_HEAD_DIM, d, d)
    offs = [0]
    for w in widths:
        offs.append(offs[-1] + w)
    cq, ckv, kr, qg, kg, vg, ga, gb = [w_in[:, offs[i]:offs[i + 1]].astype(BF16) for i in range(8)]
    parts = [_pair_apart(qg, GQA_HEADS), _pair_apart(kg, GQA_KV_HEADS), vg, cq, ckv,
             _spread_rope(kr)]
    used = sum(p.shape[1] for p in parts)
    parts.append(jnp.zeros((d, _Z_COLS - 2 * d - used), BF16))
    return ga, gb, jnp.concatenate(parts, axis=1)


def _regroup_w_uq(w_uq):
    r = w_uq.shape[0]
    w = w_uq.astype(BF16).reshape(r, MLA_HEADS, MLA_NOPE_DIM + MLA_ROPE_DIM)
    w = jnp.concatenate([w[..., :MLA_NOPE_DIM], _spread_rope(w[..., MLA_NOPE_DIM:])], axis=-1)
    return w.reshape(r, MLA_HEADS * MLA_QK_PAD)


def _regroup_w_ukv(w_ukv):
    r = w_ukv.shape[0]
    w = w_ukv.reshape(r, MLA_HEADS, MLA_NOPE_DIM + MLA_V_DIM)
    wk = w[:, :, :MLA_NOPE_DIM].reshape(r, MLA_HEADS * MLA_NOPE_DIM)
    wv = w[:, :, MLA_NOPE_DIM:].reshape(r, MLA_HEADS * MLA_V_DIM)
    return jnp.concatenate([wk, wv], axis=1).astype(BF16)


def _split_router(w_router):
    d, e = w_router.shape
    w = jnp.pad(w_router, ((0, 0), (0, LANES - e)))
    hi = w.astype(BF16)
    lo = (w - hi.astype(F32)).astype(BF16)
    return jnp.concatenate([hi, lo], axis=1)


def kernel(x, c, w_ada, b_ada, g_mix, w_in, g_cq, g_ckv, w_uq, w_ukv, g_qn, g_kn, w_br_a, w_br_b, w_out, g_ffn, w_router, w_e_gate, w_e_up, w_e_down, g_final):
    b, s, d = x.shape
    depth = w_ada.shape[0]
    cap = EC_CAPACITY * s // N_EXPERTS
    assert d == _Z_GATE_B and s % LANES == 0 and cap % LANES == 0
    rows = lambda n: min(n, s)
    tabs = _rope_tables(s)
    for l in range(depth):
        mod = _adaln(c, w_ada[l], b_ada[l]).reshape(b, 6, d)
        z = _inproj(x, mod, g_mix[l], _regroup_w_in(w_in[l]), rows(INPROJ_ROWS))
        qm, km, vm, qb, kb, vb = _prep(z, g_cq[l], g_ckv[l], _pair_apart(g_qn[l], 1),
                                       _pair_apart(g_kn[l], 1), _regroup_w_uq(w_uq[l]),
                                       _regroup_w_ukv(w_ukv[l]), tabs, rows(PREP_ROWS))
        o_a = _attention(qm, km, vm, groups=MLA_HEADS // MLA_HEADS_PER_STEP,
                         heads=MLA_HEADS_PER_STEP, dk=MLA_QK_PAD, dv=MLA_V_DIM, shared_kv=False,
                         tq=rows(ATTN_Q_ROWS), name="mla_attn")
        o_b = _attention(qb, kb, vb, groups=GQA_KV_HEADS, heads=GQA_GROUP, dk=GQA_HEAD_DIM,
                         dv=GQA_HEAD_DIM, shared_kv=True, tq=rows(ATTN_Q_ROWS), name="gqa_attn")
        wr = _split_router(w_router[l])
        x1, h2, logits = _merge(o_a, o_b, z, x, mod, g_ffn[l], w_br_a[l].astype(BF16),
                                w_br_b[l].astype(BF16), w_out[l].astype(BF16), wr,
                                rows(MERGE_ROWS))
        rank, wval, rankc = _router(logits, cap)
        xg, val = _gather(rank, wval, h2, cap)
        y = _experts(xg, val, w_e_gate[l], w_e_up[l], w_e_down[l])
        x = _combine(rankc, y, x1, mod, g_final, cap, rows(COMBINE_ROWS),
                     final=(l == depth - 1))
    return x
```

```python
import functools

import jax
import jax.numpy as jnp
from jax import lax
from jax.experimental import pallas as pl
from jax.experimental.pallas import tpu as pltpu

F32 = jnp.float32
BF16 = jnp.bfloat16

GRID_W = 64
ROPE_THETA = 10000.0
RMS_EPS = 1e-6

MLA_HEADS = 8
MLA_Q_RANK = 512
MLA_KV_RANK = 256
MLA_NOPE_DIM = 128
MLA_ROPE_DIM = 64
MLA_V_DIM = 128
MLA_QK_PAD = 256
MLA_HEADS_PER_STEP = 4
ATTN_KEY_CHUNKS = 4

GQA_HEADS = 8
GQA_KV_HEADS = 2
GQA_HEAD_DIM = 128
GQA_GROUP = GQA_HEADS // GQA_KV_HEADS

N_EXPERTS = 16
EC_CAPACITY = 2

LANES = 128

INPROJ_ROWS = 512
NORM_CHUNK = 128
PREP_ROWS = 512
ATTN_Q_ROWS = 1024
GATHER_EXPERTS_PER_STEP = 4
MERGE_ROWS = 256
COMBINE_ROWS = 256
V7X_VMEM_LIMIT = 56 * 1024 * 1024

_Z_GATE_A = 0
_Z_GATE_B = 2048
_Z_QG = 4096
_Z_KG = 5120
_Z_VG = 5376
_Z_CQ = 5632
_Z_CKV = 6144
_Z_KR = 6400
_Z_COLS = 6656
_Z_TILE = _Z_COLS // 2


def _params(sem, vmem=V7X_VMEM_LIMIT):
    return pltpu.CompilerParams(dimension_semantics=sem, vmem_limit_bytes=vmem)


def _resident(shape, index_map):
    return pl.BlockSpec(shape, index_map, pipeline_mode=pl.Buffered(1))


def _rms(x):
    return x * lax.rsqrt(jnp.mean(x * x, axis=-1, keepdims=True) + RMS_EPS)


def _rope128(x, cos, sin):
    return x * cos + pltpu.roll(x, LANES // 2, 1) * sin


def _adaln_kernel(c_ref, w_ref, b_ref, o_ref):
    c = c_ref[...]
    sc = (c * jax.nn.sigmoid(c)).astype(BF16)
    o_ref[...] = jnp.dot(sc, w_ref[...].astype(BF16), preferred_element_type=F32) + b_ref[...]


def _adaln(c, w_ada, b_ada):
    b, d = c.shape
    n = w_ada.shape[1]
    tn = 1024
    return pl.pallas_call(
        _adaln_kernel,
        grid=(n // tn,),
        in_specs=[pl.BlockSpec((b, d), lambda j: (0, 0)),
                  pl.BlockSpec((d, tn), lambda j: (0, j)),
                  pl.BlockSpec((1, tn), lambda j: (0, j))],
        out_specs=pl.BlockSpec((b, tn), lambda j: (0, j)),
        out_shape=jax.ShapeDtypeStruct((b, n), F32),
        compiler_params=_params(("arbitrary",)),
        name="adaln",
    )(c, w_ada, b_ada.reshape(1, n))


def _inproj_kernel(x_ref, mod_ref, g_ref, w_ref, z_ref, h_scr):
    @pl.when(pl.program_id(2) == 0)
    def _():
        gain = g_ref[...] * (1.0 + mod_ref[0, 1:2, :])
        shift = mod_ref[0, 0:1, :]

        def body(i, carry):
            r = pl.ds(pl.multiple_of(i * NORM_CHUNK, NORM_CHUNK), NORM_CHUNK)
            h_scr[r, :] = (_rms(x_ref[0, r, :]) * gain + shift).astype(BF16)
            return carry

        lax.fori_loop(0, h_scr.shape[0] // NORM_CHUNK, body, 0)

    z_ref[0] = jnp.dot(h_scr[...], w_ref[...], preferred_element_type=F32).astype(BF16)


def _inproj(x, mod, g_mix, w_z, tm):
    b, s, d = x.shape
    nj = _Z_COLS // _Z_TILE
    return pl.pallas_call(
        _inproj_kernel,
        grid=(b, s // tm, nj),
        in_specs=[pl.BlockSpec((1, tm, d), lambda bi, i, j: (bi, i, 0)),
                  pl.BlockSpec((1, 6, d), lambda bi, i, j: (bi, 0, 0)),
                  pl.BlockSpec((1, d), lambda bi, i, j: (0, 0)),
                  pl.BlockSpec((d, _Z_TILE), lambda bi, i, j: (0, j))],
        out_specs=pl.BlockSpec((1, tm, _Z_TILE), lambda bi, i, j: (bi, i, j)),
        out_shape=jax.ShapeDtypeStruct((b, s, _Z_COLS), BF16),
        scratch_shapes=[pltpu.VMEM((tm, d), BF16)],
        compiler_params=_params(("arbitrary", "arbitrary", "arbitrary")),
        name="inproj",
    )(x, mod, g_mix.reshape(1, d), w_z)


def _prep_kernel(cq_ref, ckv_ref, kr_ref, qg_ref, kg_ref, vg_ref,
                 gcq_ref, gckv_ref, gqn_ref, gkn_ref, wq_ref, wkv_ref,
                 c1_ref, s1_ref, c2_ref, s2_ref,
                 qm_ref, km_ref, vm_ref, qb_ref, kb_ref, vb_ref):
    c1, s1 = c1_ref[...], s1_ref[...]
    c2, s2 = c2_ref[...], s2_ref[...]
    ts = cq_ref.shape[1]
    one_col = (lax.broadcasted_iota(jnp.int32, (ts, LANES), 1) == 0).astype(BF16)

    cqn = (_rms(cq_ref[0].astype(F32)) * gcq_ref[...]).astype(BF16)
    q = jnp.dot(cqn, wq_ref[...], preferred_element_type=F32)
    q = q * ((MLA_NOPE_DIM + MLA_ROPE_DIM) ** -0.5)
    for h in range(MLA_HEADS):
        lo = h * MLA_QK_PAD
        qm_ref[0, :, lo:lo + LANES] = q[:, lo:lo + LANES].astype(BF16)
        qm_ref[0, :, lo + LANES:lo + 2 * LANES] = _rope128(
            q[:, lo + LANES:lo + 2 * LANES], c1, s1).astype(BF16)

    ckvn = (_rms(ckv_ref[0].astype(F32)) * gckv_ref[...]).astype(BF16)
    kv = jnp.dot(ckvn, wkv_ref[...], preferred_element_type=F32)
    krr = _rope128(kr_ref[0].astype(F32), c1, s1).astype(BF16)
    nk = MLA_HEADS * MLA_NOPE_DIM
    for h in range(MLA_HEADS):
        lo = h * MLA_QK_PAD
        km_ref[0, :, lo:lo + LANES] = kv[:, h * LANES:(h + 1) * LANES].astype(BF16)
        km_ref[0, :, lo + LANES:lo + 2 * LANES] = krr
    for h in range(MLA_HEADS):
        lo = 2 * h * MLA_V_DIM
        vm_ref[0, :, lo:lo + MLA_V_DIM] = kv[:, nk + h * MLA_V_DIM:nk + (h + 1) * MLA_V_DIM].astype(BF16)
        vm_ref[0, :, lo + MLA_V_DIM:lo + 2 * MLA_V_DIM] = one_col

    gqn = gqn_ref[...] * (GQA_HEAD_DIM ** -0.5)
    for h in range(GQA_HEADS):
        lo = h * GQA_HEAD_DIM
        t = _rms(qg_ref[0, :, lo:lo + GQA_HEAD_DIM].astype(F32)) * gqn
        qb_ref[0, :, lo:lo + GQA_HEAD_DIM] = _rope128(t, c2, s2).astype(BF16)
    for h in range(GQA_KV_HEADS):
        lo = h * GQA_HEAD_DIM
        t = _rms(kg_ref[0, :, lo:lo + GQA_HEAD_DIM].astype(F32)) * gkn_ref[...]
        kb_ref[0, :, lo:lo + GQA_HEAD_DIM] = _rope128(t, c2, s2).astype(BF16)
        vb_ref[0, :, 2 * lo:2 * lo + GQA_HEAD_DIM] = vg_ref[0, :, lo:lo + GQA_HEAD_DIM]
        vb_ref[0, :, 2 * lo + GQA_HEAD_DIM:2 * lo + 2 * GQA_HEAD_DIM] = one_col


def _prep(z, g_cq, g_ckv, g_qn, g_kn, wq, wkv, tabs, ts):
    b, s, _ = z.shape

    def zspec(width, col):
        return pl.BlockSpec((1, ts, width), lambda bi, i: (bi, i, col // width))

    def full(a):
        return pl.BlockSpec(a.shape, lambda bi, i: (0,) * a.ndim)

    tab_spec = pl.BlockSpec((ts, LANES), lambda bi, i: (i, 0))
    row = lambda g: g.reshape(1, -1)
    gs = [row(g_cq), row(g_ckv), row(g_qn), row(g_kn)]
    qk = MLA_HEADS * MLA_QK_PAD

    def ospec(width):
        return pl.BlockSpec((1, ts, width), lambda bi, i: (bi, i, 0))

    return pl.pallas_call(
        _prep_kernel,
        grid=(b, s // ts),
        in_specs=[zspec(MLA_Q_RANK, _Z_CQ), zspec(MLA_KV_RANK, _Z_CKV), zspec(LANES, _Z_KR),
                  zspec(GQA_HEADS * GQA_HEAD_DIM, _Z_QG), zspec(GQA_KV_HEADS * GQA_HEAD_DIM, _Z_KG),
                  zspec(GQA_KV_HEADS * GQA_HEAD_DIM, _Z_VG)]
                 + [full(g) for g in gs] + [full(wq), full(wkv)] + [tab_spec] * 4,
        out_specs=[ospec(qk), ospec(qk), ospec(2 * MLA_HEADS * MLA_V_DIM),
                   ospec(GQA_HEADS * GQA_HEAD_DIM), ospec(GQA_KV_HEADS * GQA_HEAD_DIM),
                   ospec(2 * GQA_KV_HEADS * GQA_HEAD_DIM)],
        out_shape=[jax.ShapeDtypeStruct((b, s, qk), BF16),
                   jax.ShapeDtypeStruct((b, s, qk), BF16),
                   jax.ShapeDtypeStruct((b, s, 2 * MLA_HEADS * MLA_V_DIM), BF16),
                   jax.ShapeDtypeStruct((b, s, GQA_HEADS * GQA_HEAD_DIM), BF16),
                   jax.ShapeDtypeStruct((b, s, GQA_KV_HEADS * GQA_HEAD_DIM), BF16),
                   jax.ShapeDtypeStruct((b, s, 2 * GQA_KV_HEADS * GQA_HEAD_DIM), BF16)],
        compiler_params=_params(("arbitrary", "arbitrary")),
        name="prep",
    )(z, z, z, z, z, z, *gs, wq, wkv, *tabs)


def _attn_kernel(q_ref, k_ref, v_ref, o_ref, *, heads, dk, dv, shared_kv):
    tk = k_ref.shape[1] // ATTN_KEY_CHUNKS
    for g in range(heads):
        kv = 0 if shared_kv else g
        q = q_ref[0, :, g * dk:(g + 1) * dk]
        m = acc = None
        for c in range(ATTN_KEY_CHUNKS):
            k = k_ref[0, c * tk:(c + 1) * tk, kv * dk:(kv + 1) * dk]
            v = v_ref[0, c * tk:(c + 1) * tk, 2 * kv * dv:2 * (kv + 1) * dv]
            s = lax.dot_general(q, k, (((1,), (1,)), ((), ())), preferred_element_type=F32)
            mc = jnp.max(s, axis=-1, keepdims=True)
            m_new = mc if m is None else jnp.maximum(m, mc)
            pv = jnp.dot(jnp.exp(s - m_new).astype(BF16), v, preferred_element_type=F32)
            acc = pv if acc is None else acc * jnp.exp(m - m_new) + pv
            m = m_new
        o_ref[0, :, g * dv:(g + 1) * dv] = (acc[:, :dv] / acc[:, dv:dv + 1]).astype(BF16)


def _attention(q, k, v, *, groups, heads, dk, dv, shared_kv, tq, name):
    b, s, _ = q.shape
    kvh = 1 if shared_kv else heads
    return pl.pallas_call(
        functools.partial(_attn_kernel, heads=heads, dk=dk, dv=dv, shared_kv=shared_kv),
        grid=(b, groups, s // tq),
        in_specs=[pl.BlockSpec((1, tq, heads * dk), lambda bi, g, i: (bi, i, g)),
                  pl.BlockSpec((1, s, kvh * dk), lambda bi, g, i: (bi, 0, g)),
                  pl.BlockSpec((1, s, kvh * 2 * dv), lambda bi, g, i: (bi, 0, g))],
        out_specs=pl.BlockSpec((1, tq, heads * dv), lambda bi, g, i: (bi, i, g)),
        out_shape=jax.ShapeDtypeStruct((b, s, groups * heads * dv), BF16),
        compiler_params=_params(("arbitrary", "arbitrary", "arbitrary")),
        name=name,
    )(q, k, v)


def _merge_kernel(oa_ref, ob_ref, ga_ref, gb_ref, x_ref, mod_ref, gffn_ref,
                  wa_ref, wb_ref, wo_ref, wr_ref,
                  x1_ref, h2_ref, lg_ref):
    a = jnp.dot(oa_ref[0], wa_ref[...], preferred_element_type=F32)
    bb = jnp.dot(ob_ref[0], wb_ref[...], preferred_element_type=F32)
    merged = (jax.nn.sigmoid(ga_ref[0].astype(F32)) * a
              + jax.nn.sigmoid(gb_ref[0].astype(F32)) * bb).astype(BF16)
    y = jnp.dot(merged, wo_ref[...], preferred_element_type=F32)
    x1 = x_ref[0] + mod_ref[0, 2:3, :] * y
    x1_ref[0] = x1
    h2 = _rms(x1) * gffn_ref[...] * (1.0 + mod_ref[0, 4:5, :]) + mod_ref[0, 3:4, :]
    hi = h2.astype(BF16)
    lo = (h2 - hi.astype(F32)).astype(BF16)
    h2_ref[0] = hi
    lg = jnp.dot(jnp.concatenate([hi, lo], axis=0), wr_ref[...], preferred_element_type=F32)
    tm = hi.shape[0]
    lg = lg[:tm] + lg[tm:]
    lg_ref[0] = lg[:, :LANES] + lg[:, LANES:]


def _merge(o_a, o_b, z, x, mod, g_ffn, wa, wb, wo, wr, tm):
    b, s, d = x.shape
    na, nb = o_a.shape[2], o_b.shape[2]
    tok = lambda width, col=0: pl.BlockSpec((1, tm, width), lambda bi, i: (bi, i, col // width))
    res = lambda a: _resident(a.shape, lambda bi, i: (0,) * a.ndim)
    return pl.pallas_call(
        _merge_kernel,
        grid=(b, s // tm),
        in_specs=[tok(na), tok(nb), tok(d, _Z_GATE_A), tok(d, _Z_GATE_B), tok(d),
                  pl.BlockSpec((1, 6, d), lambda bi, i: (bi, 0, 0)),
                  pl.BlockSpec((1, d), lambda bi, i: (0, 0)),
                  res(wa), res(wb), res(wo), res(wr)],
        out_specs=[tok(d), tok(d), tok(LANES)],
        out_shape=[jax.ShapeDtypeStruct((b, s, d), F32),
                   jax.ShapeDtypeStruct((b, s, d), BF16),
                   jax.ShapeDtypeStruct((b, s, LANES), F32)],
        compiler_params=_params(("arbitrary", "arbitrary")),
        name="merge",
    )(o_a, o_b, z, z, x, mod, g_ffn.reshape(1, d), wa, wb, wo, wr)


def _lane_cumsum(x):
    n = x.shape[1]
    lane = lax.broadcasted_iota(jnp.int32, x.shape, 1)
    d = 1
    while d < n:
        x = x + jnp.where(lane >= d, pltpu.roll(x, d, 1), 0.0)
        d *= 2
    return x


def _router_kernel(lg_ref, rank_ref, wval_ref, rankc_ref, *, cap):
    e = N_EXPERTS
    lt = jnp.transpose(lg_ref[0])[:e, :]
    m = jnp.max(lt, axis=0, keepdims=True)
    ex = jnp.exp(lt - m)
    aff = ex / jnp.sum(ex, axis=0, keepdims=True)

    def step(_, carry):
        lo, hi = carry
        mid = lo + ((hi - lo) >> 1)
        cnt = jnp.sum((aff >= pltpu.bitcast(mid, F32)).astype(jnp.int32), axis=1, keepdims=True)
        ok = cnt >= cap
        return jnp.where(ok, mid, lo), jnp.where(ok, hi, mid)

    lo0 = jnp.zeros((e, 1), jnp.int32)
    hi0 = jnp.full((e, 1), 0x3F800001, jnp.int32)
    thr_bits, _ = lax.fori_loop(0, 31, step, (lo0, hi0))
    thr = pltpu.bitcast(thr_bits, F32)

    gt = aff > thr
    eq = aff == thr
    need = (cap - jnp.sum(gt.astype(jnp.int32), axis=1, keepdims=True)).astype(F32)
    ceq = _lane_cumsum(eq.astype(F32))
    sel = gt | (eq & (ceq <= need))
    rank = jnp.where(sel, _lane_cumsum(sel.astype(F32)) - 1.0, -1.0)
    rank_ref[0] = rank
    wval_ref[0] = jnp.where(sel, aff, 0.0)
    pad = jnp.full((LANES - e, rank.shape[1]), -1.0, F32)
    rankc_ref[0] = jnp.transpose(jnp.concatenate([rank, pad], axis=0))


def _router(logits, cap):
    b, s, _ = logits.shape
    e = N_EXPERTS
    return pl.pallas_call(
        functools.partial(_router_kernel, cap=cap),
        grid=(b,),
        in_specs=[pl.BlockSpec((1, s, LANES), lambda bi: (bi, 0, 0))],
        out_specs=[pl.BlockSpec((1, e, s), lambda bi: (bi, 0, 0)),
                   pl.BlockSpec((1, e, s), lambda bi: (bi, 0, 0)),
                   pl.BlockSpec((1, s, LANES), lambda bi: (bi, 0, 0))],
        out_shape=[jax.ShapeDtypeStruct((b, e, s), F32),
                   jax.ShapeDtypeStruct((b, e, s), F32),
                   jax.ShapeDtypeStruct((b, s, LANES), F32)],
        compiler_params=_params(("arbitrary",)),
        name="router",
    )(logits)


def _gather_kernel(rank_ref, wval_ref, h2_ref, xg_ref, val_ref, *, cap):
    s = rank_ref.shape[3]
    r_iota = lax.broadcasted_iota(jnp.int32, (cap, s), 0).astype(F32)
    for k in range(GATHER_EXPERTS_PER_STEP):
        hot = rank_ref[0, k] == r_iota
        xg_ref[0, k] = jnp.dot(hot.astype(BF16), h2_ref[0],
                               preferred_element_type=F32).astype(BF16)
        val_ref[0, k] = jnp.sum(jnp.where(hot, wval_ref[0, k], 0.0), axis=1, keepdims=True)


def _gather(rank, wval, h2, cap):
    b, e, s = rank.shape
    d = h2.shape[2]
    ge = GATHER_EXPERTS_PER_STEP
    assert e % ge == 0
    r4 = rank.reshape(b, e, 1, s)
    w4 = wval.reshape(b, e, 1, s)
    row = pl.BlockSpec((1, ge, 1, s), lambda bi, ei: (bi, ei, 0, 0))
    return pl.pallas_call(
        functools.partial(_gather_kernel, cap=cap),
        grid=(b, e // ge),
        in_specs=[row, row, pl.BlockSpec((1, s, d), lambda bi, ei: (bi, 0, 0))],
        out_specs=[pl.BlockSpec((1, ge, cap, d), lambda bi, ei: (bi, ei, 0, 0)),
                   pl.BlockSpec((1, ge, cap, 1), lambda bi, ei: (bi, ei, 0, 0))],
        out_shape=[jax.ShapeDtypeStruct((b, e, cap, d), BF16),
                   jax.ShapeDtypeStruct((b, e, cap, 1), F32)],
        compiler_params=_params(("arbitrary", "arbitrary")),
        name="gather",
    )(r4, w4, h2)


def _expert_kernel(xg_ref, val_ref, wg_ref, wu_ref, wd_ref, y_ref, wg_s, wu_s, wd_s, *, n_exp, n_b):
    p = pl.program_id(0)
    bi = pl.program_id(1)
    ng = n_b // 2
    rows = wd_ref.shape[1]
    stage = p % 2
    ready = 1 - stage

    @pl.when(p < n_exp)
    def _():
        @pl.when(bi < ng)
        def _():
            wg_s[stage, bi] = wg_ref[0].astype(BF16)

        @pl.when(bi >= ng)
        def _():
            wu_s[stage, bi - ng] = wu_ref[0].astype(BF16)

        wd_s[stage, pl.ds(pl.multiple_of(bi * rows, rows), rows), :] = wd_ref[0].astype(BF16)

    @pl.when(p > 0)
    def _():
        xg = xg_ref[0, 0]
        a = jnp.concatenate([jnp.dot(xg, wg_s[ready, c], preferred_element_type=F32)
                             for c in range(ng)], axis=1)
        u = jnp.concatenate([jnp.dot(xg, wu_s[ready, c], preferred_element_type=F32)
                             for c in range(ng)], axis=1)
        hm = (a * jax.nn.sigmoid(a) * u).astype(BF16)
        y = jnp.dot(hm, wd_s[ready], preferred_element_type=F32)
        y_ref[0, 0] = (y * val_ref[0, 0]).astype(BF16)


def _experts(xg, val, wg, wu, wd):
    b, e, cap, d = xg.shape
    f = wg.shape[2]
    assert b % 2 == 0 and f % (b // 2) == 0 and f % b == 0
    ng = b // 2
    fc, fr = f // ng, f // b
    last = e - 1
    tok = lambda p, bi: (jnp.where(p == 0, 0, bi), jnp.maximum(p - 1, 0), 0, 0)
    exp = lambda p: jnp.minimum(p, last)
    return pl.pallas_call(
        functools.partial(_expert_kernel, n_exp=e, n_b=b),
        grid=(e + 1, b),
        in_specs=[pl.BlockSpec((1, 1, cap, d), tok),
                  pl.BlockSpec((1, 1, cap, 1), tok),
                  pl.BlockSpec((1, d, fc), lambda p, bi: (
                      exp(p), 0, jnp.where(p <= last, jnp.minimum(bi, ng - 1), ng - 1))),
                  pl.BlockSpec((1, d, fc), lambda p, bi: (
                      exp(p), 0, jnp.where(p <= last, jnp.maximum(bi - ng, 0), ng - 1))),
                  pl.BlockSpec((1, fr, d), lambda p, bi: (
                      exp(p), jnp.where(p <= last, bi, b - 1), 0))],
        out_specs=pl.BlockSpec((1, 1, cap, d), tok),
        out_shape=jax.ShapeDtypeStruct((b, e, cap, d), BF16),
        scratch_shapes=[pltpu.VMEM((2, ng, d, fc), BF16),
                        pltpu.VMEM((2, ng, d, fc), BF16),
                        pltpu.VMEM((2, f, d), BF16)],
        compiler_params=_params(("arbitrary", "arbitrary")),
        name="experts",
    )(xg, val, wg, wu, wd)


def _combine_kernel(rankc_ref, y_ref, x1_ref, mod_ref, gfin_ref, o_ref, *, cap, final):
    tm = rankc_ref.shape[1]
    rc = rankc_ref[0]
    col = lax.broadcasted_iota(jnp.int32, (tm, cap), 1).astype(F32)
    hot = jnp.concatenate(
        [(rc[:, e:e + 1] == col).astype(BF16) for e in range(N_EXPERTS)], axis=1)
    moe = jnp.dot(hot, y_ref[0], preferred_element_type=F32)
    x2 = x1_ref[0] + mod_ref[0, 5:6, :] * moe
    if final:
        x2 = _rms(x2) * gfin_ref[...]
    o_ref[0] = x2


def _combine(rankc, y, x1, mod, g_final, cap, tm, final):
    b, s, d = x1.shape
    y2 = y.reshape(b, N_EXPERTS * cap, d)
    tok = lambda width: pl.BlockSpec((1, tm, width), lambda bi, i: (bi, i, 0))
    return pl.pallas_call(
        functools.partial(_combine_kernel, cap=cap, final=final),
        grid=(b, s // tm),
        in_specs=[tok(LANES),
                  pl.BlockSpec((1, N_EXPERTS * cap, d), lambda bi, i: (bi, 0, 0)),
                  tok(d),
                  pl.BlockSpec((1, 6, d), lambda bi, i: (bi, 0, 0)),
                  pl.BlockSpec((1, d), lambda bi, i: (0, 0))],
        out_specs=tok(d),
        out_shape=jax.ShapeDtypeStruct((b, s, d), F32),
        compiler_params=_params(("arbitrary", "arbitrary")),
        name="combine",
    )(rankc, y2, x1, mod, g_final.reshape(1, d))


def _rope_tables(s):
    half = MLA_ROPE_DIM // 2
    freqs = ROPE_THETA ** (-jnp.arange(half, dtype=F32) / half)
    t = jnp.arange(s)
    ang = lambda pos: pos.astype(F32)[:, None] * freqs[None, :]
    a1, ar, ac = ang(t), ang(t // GRID_W), ang(t % GRID_W)
    z = jnp.zeros((s, half), F32)
    cat = lambda *p: jnp.concatenate(p, axis=1)
    c1 = cat(jnp.cos(a1), z, jnp.cos(a1), z)
    s1 = cat(-jnp.sin(a1), z, jnp.sin(a1), z)
    c2 = cat(jnp.cos(ar), jnp.cos(ac), jnp.cos(ar), jnp.cos(ac))
    s2 = cat(-jnp.sin(ar), -jnp.sin(ac), jnp.sin(ar), jnp.sin(ac))
    return c1, s1, c2, s2


def _pair_apart(w, heads):
    lead = w.shape[:-1]
    w = w.reshape(*lead, heads, 4, GQA_HEAD_DIM // 4)
    return w[..., jnp.array([0, 2, 1, 3]), :].reshape(*lead, heads * GQA_HEAD_DIM)


def _spread_rope(w):
    half = MLA_ROPE_DIM // 2
    z = jnp.zeros(w.shape[:-1] + (half,), w.dtype)
    return jnp.concatenate([w[..., :half], z, w[..., half:], z], axis=-1)


def _regroup_w_in(w_in):
    d = w_in.shape[0]
    widths = (MLA_Q_RANK, MLA_KV_RANK, MLA_ROPE_DIM, GQA_HEADS * GQA_HEAD_DIM,
              GQA_KV_HEADS * GQA_HEAD_DIM, GQA_KV_HEADS * GQA_HEAD_DIM, d, d)
    offs = [0]
    for w in widths:
        offs.append(offs[-1] + w)
    cq, ckv, kr, qg, kg, vg, ga, gb = [w_in[:, offs[i]:offs[i + 1]].astype(BF16) for i in range(8)]
    parts = [ga, gb, _pair_apart(qg, GQA_HEADS), _pair_apart(kg, GQA_KV_HEADS), vg, cq, ckv,
             _spread_rope(kr)]
    used = sum(p.shape[1] for p in parts)
    parts.append(jnp.zeros((d, _Z_COLS - used), BF16))
    return jnp.concatenate(parts, axis=1)


def _regroup_w_uq(w_uq):
    r = w_uq.shape[0]
    w = w_uq.astype(BF16).reshape(r, MLA_HEADS, MLA_NOPE_DIM + MLA_ROPE_DIM)
    w = jnp.concatenate([w[..., :MLA_NOPE_DIM], _spread_rope(w[..., MLA_NOPE_DIM:])], axis=-1)
    return w.reshape(r, MLA_HEADS * MLA_QK_PAD)


def _regroup_w_ukv(w_ukv):
    r = w_ukv.shape[0]
    w = w_ukv.reshape(r, MLA_HEADS, MLA_NOPE_DIM + MLA_V_DIM)
    wk = w[:, :, :MLA_NOPE_DIM].reshape(r, MLA_HEADS * MLA_NOPE_DIM)
    wv = w[:, :, MLA_NOPE_DIM:].reshape(r, MLA_HEADS * MLA_V_DIM)
    return jnp.concatenate([wk, wv], axis=1).astype(BF16)


def _split_router(w_router):
    d, e = w_router.shape
    w = jnp.pad(w_router, ((0, 0), (0, LANES - e)))
    hi = w.astype(BF16)
    lo = (w - hi.astype(F32)).astype(BF16)
    return jnp.concatenate([hi, lo], axis=1)


def kernel(x, c, w_ada, b_ada, g_mix, w_in, g_cq, g_ckv, w_uq, w_ukv, g_qn, g_kn, w_br_a, w_br_b, w_out, g_ffn, w_router, w_e_gate, w_e_up, w_e_down, g_final):
    b, s, d = x.shape
    depth = w_ada.shape[0]
    cap = EC_CAPACITY * s // N_EXPERTS
    assert d == _Z_GATE_B and s % LANES == 0 and cap % LANES == 0
    rows = lambda n: min(n, s)
    tabs = _rope_tables(s)
    for l in range(depth):
        mod = _adaln(c, w_ada[l], b_ada[l]).reshape(b, 6, d)
        z = _inproj(x, mod, g_mix[l], _regroup_w_in(w_in[l]), rows(INPROJ_ROWS))
        qm, km, vm, qb, kb, vb = _prep(z, g_cq[l], g_ckv[l], _pair_apart(g_qn[l], 1),
                                       _pair_apart(g_kn[l], 1), _regroup_w_uq(w_uq[l]),
                                       _regroup_w_ukv(w_ukv[l]), tabs, rows(PREP_ROWS))
        o_a = _attention(qm, km, vm, groups=MLA_HEADS // MLA_HEADS_PER_STEP,
                         heads=MLA_HEADS_PER_STEP, dk=MLA_QK_PAD, dv=MLA_V_DIM, shared_kv=False,
                         tq=rows(ATTN_Q_ROWS), name="mla_attn")
        o_b = _attention(qb, kb, vb, groups=GQA_KV_HEADS, heads=GQA_GROUP, dk=GQA_HEAD_DIM,
                         dv=GQA_HEAD_DIM, shared_kv=True, tq=rows(ATTN_Q_ROWS), name="gqa_attn")
        wr = _split_router(w_router[l])
        x1, h2, logits = _merge(o_a, o_b, z, x, mod, g_ffn[l], w_br_a[l].astype(BF16),
                                w_br_b[l].astype(BF16), w_out[l].astype(BF16), wr,
                                rows(MERGE_ROWS))
        rank, wval, rankc = _router(logits, cap)
        xg, val = _gather(rank, wval, h2, cap)
        y = _experts(xg, val, w_e_gate[l], w_e_up[l], w_e_down[l])
        x = _combine(rankc, y, x1, mod, g_final, cap, rows(COMBINE_ROWS),
                     final=(l == depth - 1))
    return x
```

```python
import functools

import jax
import jax.numpy as jnp
from jax import lax
from jax.experimental import pallas as pl
from jax.experimental.pallas import tpu as pltpu

F32 = jnp.float32
BF16 = jnp.bfloat16

GRID_W = 64
ROPE_THETA = 10000.0
RMS_EPS = 1e-6

MLA_HEADS = 8
MLA_Q_RANK = 512
MLA_KV_RANK = 256
MLA_NOPE_DIM = 128
MLA_ROPE_DIM = 64
MLA_V_DIM = 128
MLA_QK_PAD = 256
MLA_HEADS_PER_STEP = 4
ATTN_KEY_CHUNKS = 4

GQA_HEADS = 8
GQA_KV_HEADS = 2
GQA_HEAD_DIM = 128
GQA_GROUP = GQA_HEADS // GQA_KV_HEADS

N_EXPERTS = 16
EC_CAPACITY = 2

LANES = 128

INPROJ_ROWS = 512
NORM_CHUNK = 128
PREP_ROWS = 512
ATTN_Q_ROWS = 1024
GATHER_EXPERTS_PER_STEP = 4
MERGE_ROWS = 256
COMBINE_ROWS = 256
V7X_VMEM_LIMIT = 56 * 1024 * 1024

_Z_GATE_A = 0
_Z_GATE_B = 2048
_Z_QG = 4096
_Z_KG = 5120
_Z_VG = 5376
_Z_CQ = 5632
_Z_CKV = 6144
_Z_KR = 6400
_Z_COLS = 6656
_Z_TILE = _Z_COLS // 2


def _params(sem, vmem=V7X_VMEM_LIMIT):
    return pltpu.CompilerParams(dimension_semantics=sem, vmem_limit_bytes=vmem)


def _resident(shape, index_map):
    return pl.BlockSpec(shape, index_map, pipeline_mode=pl.Buffered(1))


def _rms(x):
    return x * lax.rsqrt(jnp.mean(x * x, axis=-1, keepdims=True) + RMS_EPS)


def _rope128(x, cos, sin):
    return x * cos + pltpu.roll(x, LANES // 2, 1) * sin


def _adaln_kernel(c_ref, w_ref, b_ref, o_ref):
    c = c_ref[...]
    sc = (c * jax.nn.sigmoid(c)).astype(BF16)
    o_ref[...] = jnp.dot(sc, w_ref[...].astype(BF16), preferred_element_type=F32) + b_ref[...]


def _adaln(c, w_ada, b_ada):
    b, d = c.shape
    n = w_ada.shape[1]
    tn = 1024
    return pl.pallas_call(
        _adaln_kernel,
        grid=(n // tn,),
        in_specs=[pl.BlockSpec((b, d), lambda j: (0, 0)),
                  pl.BlockSpec((d, tn), lambda j: (0, j)),
                  pl.BlockSpec((1, tn), lambda j: (0, j))],
        out_specs=pl.BlockSpec((b, tn), lambda j: (0, j)),
        out_shape=jax.ShapeDtypeStruct((b, n), F32),
        compiler_params=_params(("arbitrary",)),
        name="adaln",
    )(c, w_ada, b_ada.reshape(1, n))


def _inproj_kernel(x_ref, mod_ref, g_ref, w_ref, z_ref, h_scr):
    @pl.when(pl.program_id(2) == 0)
    def _():
        gain = g_ref[...] * (1.0 + mod_ref[0, 1:2, :])
        shift = mod_ref[0, 0:1, :]

        def body(i, carry):
            r = pl.ds(pl.multiple_of(i * NORM_CHUNK, NORM_CHUNK), NORM_CHUNK)
            h_scr[r, :] = (_rms(x_ref[0, r, :]) * gain + shift).astype(BF16)
            return carry

        lax.fori_loop(0, h_scr.shape[0] // NORM_CHUNK, body, 0)

    z_ref[0] = jnp.dot(h_scr[...], w_ref[...], preferred_element_type=F32).astype(BF16)


def _inproj(x, mod, g_mix, w_z, tm):
    b, s, d = x.shape
    nj = _Z_COLS // _Z_TILE
    return pl.pallas_call(
        _inproj_kernel,
        grid=(b, s // tm, nj),
        in_specs=[pl.BlockSpec((1, tm, d), lambda bi, i, j: (bi, i, 0)),
                  pl.BlockSpec((1, 6, d), lambda bi, i, j: (bi, 0, 0)),
                  pl.BlockSpec((1, d), lambda bi, i, j: (0, 0)),
                  pl.BlockSpec((d, _Z_TILE), lambda bi, i, j: (0, j))],
        out_specs=pl.BlockSpec((1, tm, _Z_TILE), lambda bi, i, j: (bi, i, j)),
        out_shape=jax.ShapeDtypeStruct((b, s, _Z_COLS), BF16),
        scratch_shapes=[pltpu.VMEM((tm, d), BF16)],
        compiler_params=_params(("arbitrary", "arbitrary", "arbitrary")),
        name="inproj",
    )(x, mod, g_mix.reshape(1, d), w_z)


def _prep_kernel(cq_ref, ckv_ref, kr_ref, qg_ref, kg_ref, vg_ref,
                 gcq_ref, gckv_ref, gqn_ref, gkn_ref, wq_ref, wkv_ref,
                 c1_ref, s1_ref, c2_ref, s2_ref,
                 qm_ref, km_ref, vm_ref, qb_ref, kb_ref, vb_ref):
    c1, s1 = c1_ref[...], s1_ref[...]
    c2, s2 = c2_ref[...], s2_ref[...]
    ts = cq_ref.shape[1]
    one_col = (lax.broadcasted_iota(jnp.int32, (ts, LANES), 1) == 0).astype(BF16)

    cqn = (_rms(cq_ref[0].astype(F32)) * gcq_ref[...]).astype(BF16)
    q = jnp.dot(cqn, wq_ref[...], preferred_element_type=F32)
    q = q * ((MLA_NOPE_DIM + MLA_ROPE_DIM) ** -0.5)
    for h in range(MLA_HEADS):
        lo = h * MLA_QK_PAD
        qm_ref[0, :, lo:lo + LANES] = q[:, lo:lo + LANES].astype(BF16)
        qm_ref[0, :, lo + LANES:lo + 2 * LANES] = _rope128(
            q[:, lo + LANES:lo + 2 * LANES], c1, s1).astype(BF16)

    ckvn = (_rms(ckv_ref[0].astype(F32)) * gckv_ref[...]).astype(BF16)
    kv = jnp.dot(ckvn, wkv_ref[...], preferred_element_type=F32)
    krr = _rope128(kr_ref[0].astype(F32), c1, s1).astype(BF16)
    nk = MLA_HEADS * MLA_NOPE_DIM
    for h in range(MLA_HEADS):
        lo = h * MLA_QK_PAD
        km_ref[0, :, lo:lo + LANES] = kv[:, h * LANES:(h + 1) * LANES].astype(BF16)
        km_ref[0, :, lo + LANES:lo + 2 * LANES] = krr
    for h in range(MLA_HEADS):
        lo = 2 * h * MLA_V_DIM
        vm_ref[0, :, lo:lo + MLA_V_DIM] = kv[:, nk + h * MLA_V_DIM:nk + (h + 1) * MLA_V_DIM].astype(BF16)
        vm_ref[0, :, lo + MLA_V_DIM:lo + 2 * MLA_V_DIM] = one_col

    gqn = gqn_ref[...] * (GQA_HEAD_DIM ** -0.5)
    for h in range(GQA_HEADS):
        lo = h * GQA_HEAD_DIM
        t = _rms(qg_ref[0, :, lo:lo + GQA_HEAD_DIM].astype(F32)) * gqn
        qb_ref[0, :, lo:lo + GQA_HEAD_DIM] = _rope128(t, c2, s2).astype(BF16)
    for h in range(GQA_KV_HEADS):
        lo = h * GQA_HEAD_DIM
        t = _rms(kg_ref[0, :, lo:lo + GQA_HEAD_DIM].astype(F32)) * gkn_ref[...]
        kb_ref[0, :, lo:lo + GQA_HEAD_DIM] = _rope128(t, c2, s2).astype(BF16)
        vb_ref[0, :, 2 * lo:2 * lo + GQA_HEAD_DIM] = vg_ref[0, :, lo:lo + GQA_HEAD_DIM]
        vb_ref[0, :, 2 * lo + GQA_HEAD_DIM:2 * lo + 2 * GQA_HEAD_DIM] = one_col


def _prep(z, g_cq, g_ckv, g_qn, g_kn, wq, wkv, tabs, ts):
    b, s, _ = z.shape

    def zspec(width, col):
        return pl.BlockSpec((1, ts, width), lambda bi, i: (bi, i, col // width))

    def full(a):
        return pl.BlockSpec(a.shape, lambda bi, i: (0,) * a.ndim)

    tab_spec = pl.BlockSpec((ts, LANES), lambda bi, i: (i, 0))
    row = lambda g: g.reshape(1, -1)
    gs = [row(g_cq), row(g_ckv), row(g_qn), row(g_kn)]
    qk = MLA_HEADS * MLA_QK_PAD

    def ospec(width):
        return pl.BlockSpec((1, ts, width), lambda bi, i: (bi, i, 0))

    return pl.pallas_call(
        _prep_kernel,
        grid=(b, s // ts),
        in_specs=[zspec(MLA_Q_RANK, _Z_CQ), zspec(MLA_KV_RANK, _Z_CKV), zspec(LANES, _Z_KR),
                  zspec(GQA_HEADS * GQA_HEAD_DIM, _Z_QG), zspec(GQA_KV_HEADS * GQA_HEAD_DIM, _Z_KG),
                  zspec(GQA_KV_HEADS * GQA_HEAD_DIM, _Z_VG)]
                 + [full(g) for g in gs] + [full(wq), full(wkv)] + [tab_spec] * 4,
        out_specs=[ospec(qk), ospec(qk), ospec(2 * MLA_HEADS * MLA_V_DIM),
                   ospec(GQA_HEADS * GQA_HEAD_DIM), ospec(GQA_KV_HEADS * GQA_HEAD_DIM),
                   ospec(2 * GQA_KV_HEADS * GQA_HEAD_DIM)],
        out_shape=[jax.ShapeDtypeStruct((b, s, qk), BF16),
                   jax.ShapeDtypeStruct((b, s, qk), BF16),
                   jax.ShapeDtypeStruct((b, s, 2 * MLA_HEADS * MLA_V_DIM), BF16),
                   jax.ShapeDtypeStruct((b, s, GQA_HEADS * GQA_HEAD_DIM), BF16),
                   jax.ShapeDtypeStruct((b, s, GQA_KV_HEADS * GQA_HEAD_DIM), BF16),
                   jax.ShapeDtypeStruct((b, s, 2 * GQA_KV_HEADS * GQA_HEAD_DIM), BF16)],
        compiler_params=_params(("arbitrary", "arbitrary")),
        name="prep",
    )(z, z, z, z, z, z, *gs, wq, wkv, *tabs)


def _attn_kernel(q_ref, k_ref, v_ref, o_ref, *, heads, dk, dv, shared_kv):
    tk = k_ref.shape[1] // ATTN_KEY_CHUNKS
    for g in range(heads):
        kv = 0 if shared_kv else g
        q = q_ref[0, :, g * dk:(g + 1) * dk]
        m = acc = None
        for c in range(ATTN_KEY_CHUNKS):
            k = k_ref[0, c * tk:(c + 1) * tk, kv * dk:(kv + 1) * dk]
            v = v_ref[0, c * tk:(c + 1) * tk, 2 * kv * dv:2 * (kv + 1) * dv]
            s = lax.dot_general(q, k, (((1,), (1,)), ((), ())), preferred_element_type=F32)
            mc = jnp.max(s, axis=-1, keepdims=True)
            m_new = mc if m is None else jnp.maximum(m, mc)
            pv = jnp.dot(jnp.exp(s - m_new).astype(BF16), v, preferred_element_type=F32)
            acc = pv if acc is None else acc * jnp.exp(m - m_new) + pv
            m = m_new
        o_ref[0, :, g * dv:(g + 1) * dv] = (acc[:, :dv] / acc[:, dv:dv + 1]).astype(BF16)


def _attention(q, k, v, *, groups, heads, dk, dv, shared_kv, tq, name):
    b, s, _ = q.shape
    kvh = 1 if shared_kv else heads
    return pl.pallas_call(
        functools.partial(_attn_kernel, heads=heads, dk=dk, dv=dv, shared_kv=shared_kv),
        grid=(b, groups, s // tq),
        in_specs=[pl.BlockSpec((1, tq, heads * dk), lambda bi, g, i: (bi, i, g)),
                  pl.BlockSpec((1, s, kvh * dk), lambda bi, g, i: (bi, 0, g)),
                  pl.BlockSpec((1, s, kvh * 2 * dv), lambda bi, g, i: (bi, 0, g))],
        out_specs=pl.BlockSpec((1, tq, heads * dv), lambda bi, g, i: (bi, i, g)),
        out_shape=jax.ShapeDtypeStruct((b, s, groups * heads * dv), BF16),
        compiler_params=_params(("arbitrary", "arbitrary", "arbitrary")),
        name=name,
    )(q, k, v)


def _merge_kernel(oa_ref, ob_ref, ga_ref, gb_ref, x_ref, mod_ref, gffn_ref,
                  wa_ref, wb_ref, wo_ref, wr_ref,
                  x1_ref, h2_ref, lg_ref):
    a = jnp.dot(oa_ref[0], wa_ref[...], preferred_element_type=F32)
    bb = jnp.dot(ob_ref[0], wb_ref[...], preferred_element_type=F32)
    merged = (jax.nn.sigmoid(ga_ref[0].astype(F32)) * a
              + jax.nn.sigmoid(gb_ref[0].astype(F32)) * bb).astype(BF16)
    y = jnp.dot(merged, wo_ref[...], preferred_element_type=F32)
    x1 = x_ref[0] + mod_ref[0, 2:3, :] * y
    x1_ref[0] = x1
    h2 = _rms(x1) * gffn_ref[...] * (1.0 + mod_ref[0, 4:5, :]) + mod_ref[0, 3:4, :]
    hi = h2.astype(BF16)
    lo = (h2 - hi.astype(F32)).astype(BF16)
    h2_ref[0] = hi
    lg = jnp.dot(jnp.concatenate([hi, lo], axis=0), wr_ref[...], preferred_element_type=F32)
    tm = hi.shape[0]
    lg = lg[:tm] + lg[tm:]
    lg_ref[0] = lg[:, :LANES] + lg[:, LANES:]


def _merge(o_a, o_b, z, x, mod, g_ffn, wa, wb, wo, wr, tm):
    b, s, d = x.shape
    na, nb = o_a.shape[2], o_b.shape[2]
    tok = lambda width, col=0: pl.BlockSpec((1, tm, width), lambda bi, i: (bi, i, col // width))
    res = lambda a: _resident(a.shape, lambda bi, i: (0,) * a.ndim)
    return pl.pallas_call(
        _merge_kernel,
        grid=(b, s // tm),
        in_specs=[tok(na), tok(nb), tok(d, _Z_GATE_A), tok(d, _Z_GATE_B), tok(d),
                  pl.BlockSpec((1, 6, d), lambda bi, i: (bi, 0, 0)),
                  pl.BlockSpec((1, d), lambda bi, i: (0, 0)),
                  res(wa), res(wb), res(wo), res(wr)],
        out_specs=[tok(d), tok(d), tok(LANES)],
        out_shape=[jax.ShapeDtypeStruct((b, s, d), F32),
                   jax.ShapeDtypeStruct((b, s, d), BF16),
                   jax.ShapeDtypeStruct((b, s, LANES), F32)],
        compiler_params=_params(("arbitrary", "arbitrary")),
        name="merge",
    )(o_a, o_b, z, z, x, mod, g_ffn.reshape(1, d), wa, wb, wo, wr)


def _lane_cumsum(x):
    n = x.shape[1]
    lane = lax.broadcasted_iota(jnp.int32, x.shape, 1)
    d = 1
    while d < n:
        x = x + jnp.where(lane >= d, pltpu.roll(x, d, 1), 0.0)
        d *= 2
    return x


def _router_kernel(lg_ref, rank_ref, wval_ref, rankc_ref, *, cap):
    e = N_EXPERTS
    nb, s = lg_ref.shape[:2]
    affs = []
    for bi in range(nb):
        lt = jnp.transpose(lg_ref[bi])[:e, :]
        ex = jnp.exp(lt - jnp.max(lt, axis=0, keepdims=True))
        affs.append(ex / jnp.sum(ex, axis=0, keepdims=True))
    aff = jnp.concatenate(affs, axis=0)
    rows = nb * e

    def step(_, carry):
        lo, hi = carry
        mid = lo + ((hi - lo) >> 1)
        cnt = jnp.sum((aff >= pltpu.bitcast(mid, F32)).astype(jnp.int32), axis=1, keepdims=True)
        ok = cnt >= cap
        return jnp.where(ok, mid, lo), jnp.where(ok, hi, mid)

    lo0 = jnp.zeros((rows, 1), jnp.int32)
    hi0 = jnp.full((rows, 1), 0x3F800001, jnp.int32)
    thr_bits, _ = lax.fori_loop(0, 31, step, (lo0, hi0))
    thr = pltpu.bitcast(thr_bits, F32)

    gt = aff > thr
    eq = aff == thr
    need = (cap - jnp.sum(gt.astype(jnp.int32), axis=1, keepdims=True)).astype(F32)
    ceq = _lane_cumsum(eq.astype(F32))
    sel = gt | (eq & (ceq <= need))
    rank = jnp.where(sel, _lane_cumsum(sel.astype(F32)) - 1.0, -1.0)
    wval = jnp.where(sel, aff, 0.0)
    pad = jnp.full((LANES - e, s), -1.0, F32)
    for bi in range(nb):
        rank_b = rank[bi * e:(bi + 1) * e]
        rank_ref[bi] = rank_b
        wval_ref[bi] = wval[bi * e:(bi + 1) * e]
        rankc_ref[bi] = jnp.transpose(jnp.concatenate([rank_b, pad], axis=0))


def _router(logits, cap):
    b, s, _ = logits.shape
    e = N_EXPERTS
    whole = lambda *shape: pl.BlockSpec(shape, lambda i: (0,) * len(shape))
    return pl.pallas_call(
        functools.partial(_router_kernel, cap=cap),
        grid=(1,),
        in_specs=[whole(b, s, LANES)],
        out_specs=[whole(b, e, s), whole(b, e, s), whole(b, s, LANES)],
        out_shape=[jax.ShapeDtypeStruct((b, e, s), F32),
                   jax.ShapeDtypeStruct((b, e, s), F32),
                   jax.ShapeDtypeStruct((b, s, LANES), F32)],
        compiler_params=_params(("arbitrary",)),
        name="router",
    )(logits)


def _gather_kernel(rank_ref, wval_ref, h2_ref, xg_ref, val_ref, *, cap):
    s = rank_ref.shape[3]
    r_iota = lax.broadcasted_iota(jnp.int32, (cap, s), 0).astype(F32)
    for k in range(GATHER_EXPERTS_PER_STEP):
        hot = rank_ref[0, k] == r_iota
        xg_ref[0, k] = jnp.dot(hot.astype(BF16), h2_ref[0],
                               preferred_element_type=F32).astype(BF16)
        val_ref[0, k] = jnp.sum(jnp.where(hot, wval_ref[0, k], 0.0), axis=1, keepdims=True)


def _gather(rank, wval, h2, cap):
    b, e, s = rank.shape
    d = h2.shape[2]
    ge = GATHER_EXPERTS_PER_STEP
    assert e % ge == 0
    r4 = rank.reshape(b, e, 1, s)
    w4 = wval.reshape(b, e, 1, s)
    row = pl.BlockSpec((1, ge, 1, s), lambda bi, ei: (bi, ei, 0, 0))
    return pl.pallas_call(
        functools.partial(_gather_kernel, cap=cap),
        grid=(b, e // ge),
        in_specs=[row, row, pl.BlockSpec((1, s, d), lambda bi, ei: (bi, 0, 0))],
        out_specs=[pl.BlockSpec((1, ge, cap, d), lambda bi, ei: (bi, ei, 0, 0)),
                   pl.BlockSpec((1, ge, cap, 1), lambda bi, ei: (bi, ei, 0, 0))],
        out_shape=[jax.ShapeDtypeStruct((b, e, cap, d), BF16),
                   jax.ShapeDtypeStruct((b, e, cap, 1), F32)],
        compiler_params=_params(("arbitrary", "arbitrary")),
        name="gather",
    )(r4, w4, h2)


def _expert_kernel(xg_ref, val_ref, wg_ref, wu_ref, wd_ref, y_ref, wg_s, wu_s, wd_s, *, n_exp, n_b):
    p = pl.program_id(0)
    bi = pl.program_id(1)
    ng = n_b // 2
    rows = wd_ref.shape[1]
    stage = p % 2
    ready = 1 - stage

    @pl.when(p < n_exp)
    def _():
        @pl.when(bi < ng)
        def _():
            wg_s[stage, bi] = wg_ref[0].astype(BF16)

        @pl.when(bi >= ng)
        def _():
            wu_s[stage, bi - ng] = wu_ref[0].astype(BF16)

        wd_s[stage, pl.ds(pl.multiple_of(bi * rows, rows), rows), :] = wd_ref[0].astype(BF16)

    @pl.when(p > 0)
    def _():
        xg = xg_ref[0, 0]
        a = jnp.concatenate([jnp.dot(xg, wg_s[ready, c], preferred_element_type=F32)
                             for c in range(ng)], axis=1)
        u = jnp.concatenate([jnp.dot(xg, wu_s[ready, c], preferred_element_type=F32)
                             for c in range(ng)], axis=1)
        hm = (a * jax.nn.sigmoid(a) * u).astype(BF16)
        y = jnp.dot(hm, wd_s[ready], preferred_element_type=F32)
        y_ref[0, 0] = (y * val_ref[0, 0]).astype(BF16)


def _experts(xg, val, wg, wu, wd):
    b, e, cap, d = xg.shape
    f = wg.shape[2]
    assert b % 2 == 0 and f % (b // 2) == 0 and f % b == 0
    ng = b // 2
    fc, fr = f // ng, f // b
    last = e - 1
    tok = lambda p, bi: (jnp.where(p == 0, 0, bi), jnp.maximum(p - 1, 0), 0, 0)
    exp = lambda p: jnp.minimum(p, last)
    return pl.pallas_call(
        functools.partial(_expert_kernel, n_exp=e, n_b=b),
        grid=(e + 1, b),
        in_specs=[pl.BlockSpec((1, 1, cap, d), tok),
                  pl.BlockSpec((1, 1, cap, 1), tok),
                  pl.BlockSpec((1, d, fc), lambda p, bi: (
                      exp(p), 0, jnp.where(p <= last, jnp.minimum(bi, ng - 1), ng - 1))),
                  pl.BlockSpec((1, d, fc), lambda p, bi: (
                      exp(p), 0, jnp.where(p <= last, jnp.maximum(bi - ng, 0), ng - 1))),
                  pl.BlockSpec((1, fr, d), lambda p, bi: (
                      exp(p), jnp.where(p <= last, bi, b - 1), 0))],
        out_specs=pl.BlockSpec((1, 1, cap, d), tok),
        out_shape=jax.ShapeDtypeStruct((b, e, cap, d), BF16),
        scratch_shapes=[pltpu.VMEM((2, ng, d, fc), BF16),
                        pltpu.VMEM((2, ng, d, fc), BF16),
                        pltpu.VMEM((2, f, d), BF16)],
        compiler_params=_params(("arbitrary", "arbitrary")),
        name="experts",
    )(xg, val, wg, wu, wd)


def _combine_kernel(rankc_ref, y_ref, x1_ref, mod_ref, gfin_ref, o_ref, *, cap, final):
    tm = rankc_ref.shape[1]
    rc = rankc_ref[0]
    col = lax.broadcasted_iota(jnp.int32, (tm, cap), 1).astype(F32)
    hot = jnp.concatenate(
        [(rc[:, e:e + 1] == col).astype(BF16) for e in range(N_EXPERTS)], axis=1)
    moe = jnp.dot(hot, y_ref[0], preferred_element_type=F32)
    x2 = x1_ref[0] + mod_ref[0, 5:6, :] * moe
    if final:
        x2 = _rms(x2) * gfin_ref[...]
    o_ref[0] = x2


def _combine(rankc, y, x1, mod, g_final, cap, tm, final):
    b, s, d = x1.shape
    y2 = y.reshape(b, N_EXPERTS * cap, d)
    tok = lambda width: pl.BlockSpec((1, tm, width), lambda bi, i: (bi, i, 0))
    return pl.pallas_call(
        functools.partial(_combine_kernel, cap=cap, final=final),
        grid=(b, s // tm),
        in_specs=[tok(LANES),
                  pl.BlockSpec((1, N_EXPERTS * cap, d), lambda bi, i: (bi, 0, 0)),
                  tok(d),
                  pl.BlockSpec((1, 6, d), lambda bi, i: (bi, 0, 0)),
                  pl.BlockSpec((1, d), lambda bi, i: (0, 0))],
        out_specs=tok(d),
        out_shape=jax.ShapeDtypeStruct((b, s, d), F32),
        compiler_params=_params(("arbitrary", "arbitrary")),
        name="combine",
    )(rankc, y2, x1, mod, g_final.reshape(1, d))


def _rope_tables(s):
    half = MLA_ROPE_DIM // 2
    freqs = ROPE_THETA ** (-jnp.arange(half, dtype=F32) / half)
    t = jnp.arange(s)
    ang = lambda pos: pos.astype(F32)[:, None] * freqs[None, :]
    a1, ar, ac = ang(t), ang(t // GRID_W), ang(t % GRID_W)
    z = jnp.zeros((s, half), F32)
    cat = lambda *p: jnp.concatenate(p, axis=1)
    c1 = cat(jnp.cos(a1), z, jnp.cos(a1), z)
    s1 = cat(-jnp.sin(a1), z, jnp.sin(a1), z)
    c2 = cat(jnp.cos(ar), jnp.cos(ac), jnp.cos(ar), jnp.cos(ac))
    s2 = cat(-jnp.sin(ar), -jnp.sin(ac), jnp.sin(ar), jnp.sin(ac))
    return c1, s1, c2, s2


def _pair_apart(w, heads):
    lead = w.shape[:-1]
    w = w.reshape(*lead, heads, 4, GQA_HEAD_DIM // 4)
    return w[..., jnp.array([0, 2, 1, 3]), :].reshape(*lead, heads * GQA_HEAD_DIM)


def _spread_rope(w):
    half = MLA_ROPE_DIM // 2
    z = jnp.zeros(w.shape[:-1] + (half,), w.dtype)
    return jnp.concatenate([w[..., :half], z, w[..., half:], z], axis=-1)


def _regroup_kernel(w_ref, o_ref, *, segments):
    r, n = w_ref.shape
    lane = lax.broadcasted_iota(jnp.int32, (r, LANES), 1)
    half, quarter = LANES // 2, LANES // 4

    def tile(k):
        lo = k * LANES
        if lo + LANES <= n:
            return w_ref[:, lo:lo + LANES]
        return jnp.concatenate([w_ref[:, lo:n], jnp.zeros((r, lo + LANES - n), F32)], axis=1)

    def straddling(k):
        return pltpu.roll(jnp.where(lane >= half, tile(k), tile(k + 1)), half, 1)

    def pair_apart(x):
        inner = jnp.where(lane < half, pltpu.roll(x, LANES - quarter, 1), pltpu.roll(x, quarter, 1))
        return jnp.where((lane < quarter) | (lane >= LANES - quarter), x, inner)

    def spread(x):
        moved = jnp.where((lane >= half) & (lane < half + quarter), pltpu.roll(x, quarter, 1), 0.0)
        return jnp.where(lane < quarter, x, moved)

    out = 0
    for src, width, kind in segments:
        for j in range(-(-width // LANES)):
            if kind == "zero":
                t = jnp.zeros((r, LANES), F32)
            elif src % LANES == 0:
                t = tile(src // LANES + j)
            else:
                assert src % LANES == half
                t = straddling(src // LANES + j)
            if kind == "pair_apart":
                t = pair_apart(t)
            elif kind == "spread":
                t = spread(t)
            o_ref[:, out:out + LANES] = t.astype(BF16)
            out += LANES
    assert out == o_ref.shape[1]


def _regroup_w_in(w_in):
    d, n = w_in.shape
    names = ("cq", "ckv", "kr", "qg", "kg", "vg", "ga", "gb")
    widths = (MLA_Q_RANK, MLA_KV_RANK, MLA_ROPE_DIM, GQA_HEADS * GQA_HEAD_DIM,
              GQA_KV_HEADS * GQA_HEAD_DIM, GQA_KV_HEADS * GQA_HEAD_DIM, d, d)
    src, off = {}, 0
    for name, w in zip(names, widths):
        src[name] = (off, w)
        off += w
    assert off == n
    segments = [src["ga"] + ("copy",), src["gb"] + ("copy",), src["qg"] + ("pair_apart",),
                src["kg"] + ("pair_apart",), src["vg"] + ("copy",), src["cq"] + ("copy",),
                src["ckv"] + ("copy",), src["kr"] + ("spread",)]
    used = sum(-(-w // LANES) * LANES for _, w, _ in segments)
    segments.append((0, _Z_COLS - used, "zero"))
    rows = 256
    return pl.pallas_call(
        functools.partial(_regroup_kernel, segments=tuple(segments)),
        grid=(d // rows,),
        in_specs=[pl.BlockSpec((rows, n), lambda i: (i, 0))],
        out_specs=pl.BlockSpec((rows, _Z_COLS), lambda i: (i, 0)),
        out_shape=jax.ShapeDtypeStruct((d, _Z_COLS), BF16),
        compiler_params=_params(("arbitrary",)),
        name="regroup",
    )(w_in)


def _regroup_w_uq(w_uq):
    r = w_uq.shape[0]
    w = w_uq.astype(BF16).reshape(r, MLA_HEADS, MLA_NOPE_DIM + MLA_ROPE_DIM)
    w = jnp.concatenate([w[..., :MLA_NOPE_DIM], _spread_rope(w[..., MLA_NOPE_DIM:])], axis=-1)
    return w.reshape(r, MLA_HEADS * MLA_QK_PAD)


def _regroup_w_ukv(w_ukv):
    r = w_ukv.shape[0]
    w = w_ukv.reshape(r, MLA_HEADS, MLA_NOPE_DIM + MLA_V_DIM)
    wk = w[:, :, :MLA_NOPE_DIM].reshape(r, MLA_HEADS * MLA_NOPE_DIM)
    wv = w[:, :, MLA_NOPE_DIM:].reshape(r, MLA_HEADS * MLA_V_DIM)
    return jnp.concatenate([wk, wv], axis=1).astype(BF16)


def _split_router(w_router):
    d, e = w_router.shape
    w = jnp.pad(w_router, ((0, 0), (0, LANES - e)))
    hi = w.astype(BF16)
    lo = (w - hi.astype(F32)).astype(BF16)
    return jnp.concatenate([hi, lo], axis=1)


def kernel(x, c, w_ada, b_ada, g_mix, w_in, g_cq, g_ckv, w_uq, w_ukv, g_qn, g_kn, w_br_a, w_br_b, w_out, g_ffn, w_router, w_e_gate, w_e_up, w_e_down, g_final):
    b, s, d = x.shape
    depth = w_ada.shape[0]
    cap = EC_CAPACITY * s // N_EXPERTS
    assert d == _Z_GATE_B and s % LANES == 0 and cap % LANES == 0
    rows = lambda n: min(n, s)
    tabs = _rope_tables(s)
    for l in range(depth):
        mod = _adaln(c, w_ada[l], b_ada[l]).reshape(b, 6, d)
        z = _inproj(x, mod, g_mix[l], _regroup_w_in(w_in[l]), rows(INPROJ_ROWS))
        qm, km, vm, qb, kb, vb = _prep(z, g_cq[l], g_ckv[l], _pair_apart(g_qn[l], 1),
                                       _pair_apart(g_kn[l], 1), _regroup_w_uq(w_uq[l]),
                                       _regroup_w_ukv(w_ukv[l]), tabs, rows(PREP_ROWS))
        o_a = _attention(qm, km, vm, groups=MLA_HEADS // MLA_HEADS_PER_STEP,
                         heads=MLA_HEADS_PER_STEP, dk=MLA_QK_PAD, dv=MLA_V_DIM, shared_kv=False,
                         tq=rows(ATTN_Q_ROWS), name="mla_attn")
        o_b = _attention(qb, kb, vb, groups=GQA_KV_HEADS, heads=GQA_GROUP, dk=GQA_HEAD_DIM,
                         dv=GQA_HEAD_DIM, shared_kv=True, tq=rows(ATTN_Q_ROWS), name="gqa_attn")
        wr = _split_router(w_router[l])
        x1, h2, logits = _merge(o_a, o_b, z, x, mod, g_ffn[l], w_br_a[l].astype(BF16),
                                w_br_b[l].astype(BF16), w_out[l].astype(BF16), wr,
                                rows(MERGE_ROWS))
        rank, wval, rankc = _router(logits, cap)
        xg, val = _gather(rank, wval, h2, cap)
        y = _experts(xg, val, w_e_gate[l], w_e_up[l], w_e_down[l])
        x = _combine(rankc, y, x1, mod, g_final, cap, rows(COMBINE_ROWS),
                     final=(l == depth - 1))
    return x
```

```python
import functools

import jax
import jax.numpy as jnp
from jax import lax
from jax.experimental import pallas as pl
from jax.experimental.pallas import tpu as pltpu

F32 = jnp.float32
BF16 = jnp.bfloat16

GRID_W = 64
ROPE_THETA = 10000.0
RMS_EPS = 1e-6

MLA_HEADS = 8
MLA_Q_RANK = 512
MLA_KV_RANK = 256
MLA_NOPE_DIM = 128
MLA_ROPE_DIM = 64
MLA_V_DIM = 128
MLA_QK_PAD = 256
MLA_HEADS_PER_STEP = 4
ATTN_KEY_CHUNKS = 4

GQA_HEADS = 8
GQA_KV_HEADS = 2
GQA_HEAD_DIM = 128
GQA_GROUP = GQA_HEADS // GQA_KV_HEADS

N_EXPERTS = 16
EC_CAPACITY = 2

LANES = 128

INPROJ_ROWS = 512
NORM_CHUNK = 128
PREP_ROWS = 512
ATTN_Q_ROWS = 1024
GATHER_EXPERTS_PER_STEP = 8
EXPERT_BATCHES_PER_STEP = 2
MERGE_ROWS = 256
COMBINE_ROWS = 256
V7X_VMEM_LIMIT = 56 * 1024 * 1024

_Z_GATE_A = 0
_Z_GATE_B = 2048
_Z_QG = 4096
_Z_KG = 5120
_Z_VG = 5376
_Z_CQ = 5632
_Z_CKV = 6144
_Z_KR = 6400
_Z_COLS = 6656
_Z_TILE = _Z_COLS // 2


def _params(sem, vmem=V7X_VMEM_LIMIT):
    return pltpu.CompilerParams(dimension_semantics=sem, vmem_limit_bytes=vmem)


def _resident(shape, index_map):
    return pl.BlockSpec(shape, index_map, pipeline_mode=pl.Buffered(1))


def _rms(x):
    return x * lax.rsqrt(jnp.mean(x * x, axis=-1, keepdims=True) + RMS_EPS)


def _rope128(x, cos, sin):
    return x * cos + pltpu.roll(x, LANES // 2, 1) * sin


def _adaln_kernel(c_ref, w_ref, b_ref, o_ref):
    c = c_ref[...]
    sc = (c * jax.nn.sigmoid(c)).astype(BF16)
    o_ref[...] = jnp.dot(sc, w_ref[...].astype(BF16), preferred_element_type=F32) + b_ref[...]


def _adaln(c, w_ada, b_ada):
    b, d = c.shape
    n = w_ada.shape[1]
    tn = 1024
    return pl.pallas_call(
        _adaln_kernel,
        grid=(n // tn,),
        in_specs=[pl.BlockSpec((b, d), lambda j: (0, 0)),
                  pl.BlockSpec((d, tn), lambda j: (0, j)),
                  pl.BlockSpec((1, tn), lambda j: (0, j))],
        out_specs=pl.BlockSpec((b, tn), lambda j: (0, j)),
        out_shape=jax.ShapeDtypeStruct((b, n), F32),
        compiler_params=_params(("arbitrary",)),
        name="adaln",
    )(c, w_ada, b_ada.reshape(1, n))


def _inproj_kernel(x_ref, mod_ref, g_ref, w_ref, z_ref, h_scr):
    @pl.when(pl.program_id(2) == 0)
    def _():
        gain = g_ref[...] * (1.0 + mod_ref[0, 1:2, :])
        shift = mod_ref[0, 0:1, :]

        def body(i, carry):
            r = pl.ds(pl.multiple_of(i * NORM_CHUNK, NORM_CHUNK), NORM_CHUNK)
            h_scr[r, :] = (_rms(x_ref[0, r, :]) * gain + shift).astype(BF16)
            return carry

        lax.fori_loop(0, h_scr.shape[0] // NORM_CHUNK, body, 0)

    z_ref[0] = lax.dot_general(h_scr[...], w_ref[...], (((1,), (1,)), ((), ())),
                               preferred_element_type=F32).astype(BF16)


def _inproj(x, mod, g_mix, w_z, tm):
    b, s, d = x.shape
    nj = _Z_COLS // _Z_TILE
    return pl.pallas_call(
        _inproj_kernel,
        grid=(b, s // tm, nj),
        in_specs=[pl.BlockSpec((1, tm, d), lambda bi, i, j: (bi, i, 0)),
                  pl.BlockSpec((1, 6, d), lambda bi, i, j: (bi, 0, 0)),
                  pl.BlockSpec((1, d), lambda bi, i, j: (0, 0)),
                  pl.BlockSpec((_Z_TILE, d), lambda bi, i, j: (j, 0))],
        out_specs=pl.BlockSpec((1, tm, _Z_TILE), lambda bi, i, j: (bi, i, j)),
        out_shape=jax.ShapeDtypeStruct((b, s, _Z_COLS), BF16),
        scratch_shapes=[pltpu.VMEM((tm, d), BF16)],
        compiler_params=_params(("arbitrary", "arbitrary", "arbitrary")),
        name="inproj",
    )(x, mod, g_mix.reshape(1, d), w_z)


def _prep_kernel(cq_ref, ckv_ref, kr_ref, qg_ref, kg_ref, vg_ref,
                 gcq_ref, gckv_ref, gqn_ref, gkn_ref, wq_ref, wkv_ref,
                 c1_ref, s1_ref, c2_ref, s2_ref,
                 qm_ref, km_ref, vm_ref, qb_ref, kb_ref, vb_ref):
    c1, s1 = c1_ref[...], s1_ref[...]
    c2, s2 = c2_ref[...], s2_ref[...]
    ts = cq_ref.shape[1]
    one_col = (lax.broadcasted_iota(jnp.int32, (ts, LANES), 1) == 0).astype(BF16)

    cqn = (_rms(cq_ref[0].astype(F32)) * gcq_ref[...]).astype(BF16)
    q = jnp.dot(cqn, wq_ref[...], preferred_element_type=F32)
    q = q * ((MLA_NOPE_DIM + MLA_ROPE_DIM) ** -0.5)
    for h in range(MLA_HEADS):
        lo = h * MLA_QK_PAD
        qm_ref[0, :, lo:lo + LANES] = q[:, lo:lo + LANES].astype(BF16)
        qm_ref[0, :, lo + LANES:lo + 2 * LANES] = _rope128(
            q[:, lo + LANES:lo + 2 * LANES], c1, s1).astype(BF16)

    ckvn = (_rms(ckv_ref[0].astype(F32)) * gckv_ref[...]).astype(BF16)
    kv = jnp.dot(ckvn, wkv_ref[...], preferred_element_type=F32)
    krr = _rope128(kr_ref[0].astype(F32), c1, s1).astype(BF16)
    nk = MLA_HEADS * MLA_NOPE_DIM
    for h in range(MLA_HEADS):
        lo = h * MLA_QK_PAD
        km_ref[0, :, lo:lo + LANES] = kv[:, h * LANES:(h + 1) * LANES].astype(BF16)
        km_ref[0, :, lo + LANES:lo + 2 * LANES] = krr
    for h in range(MLA_HEADS):
        lo = 2 * h * MLA_V_DIM
        vm_ref[0, :, lo:lo + MLA_V_DIM] = kv[:, nk + h * MLA_V_DIM:nk + (h + 1) * MLA_V_DIM].astype(BF16)
        vm_ref[0, :, lo + MLA_V_DIM:lo + 2 * MLA_V_DIM] = one_col

    gqn = gqn_ref[...] * (GQA_HEAD_DIM ** -0.5)
    for h in range(GQA_HEADS):
        lo = h * GQA_HEAD_DIM
        t = _rms(qg_ref[0, :, lo:lo + GQA_HEAD_DIM].astype(F32)) * gqn
        qb_ref[0, :, lo:lo + GQA_HEAD_DIM] = _rope128(t, c2, s2).astype(BF16)
    for h in range(GQA_KV_HEADS):
        lo = h * GQA_HEAD_DIM
        t = _rms(kg_ref[0, :, lo:lo + GQA_HEAD_DIM].astype(F32)) * gkn_ref[...]
        kb_ref[0, :, lo:lo + GQA_HEAD_DIM] = _rope128(t, c2, s2).astype(BF16)
        vb_ref[0, :, 2 * lo:2 * lo + GQA_HEAD_DIM] = vg_ref[0, :, lo:lo + GQA_HEAD_DIM]
        vb_ref[0, :, 2 * lo + GQA_HEAD_DIM:2 * lo + 2 * GQA_HEAD_DIM] = one_col


def _prep(z, g_cq, g_ckv, g_qn, g_kn, wq, wkv, tabs, ts):
    b, s, _ = z.shape

    def zspec(width, col):
        return pl.BlockSpec((1, ts, width), lambda bi, i: (bi, i, col // width))

    def full(a):
        return pl.BlockSpec(a.shape, lambda bi, i: (0,) * a.ndim)

    tab_spec = pl.BlockSpec((ts, LANES), lambda bi, i: (i, 0))
    row = lambda g: g.reshape(1, -1)
    gs = [row(g_cq), row(g_ckv), row(g_qn), row(g_kn)]
    qk = MLA_HEADS * MLA_QK_PAD

    def ospec(width):
        return pl.BlockSpec((1, ts, width), lambda bi, i: (bi, i, 0))

    return pl.pallas_call(
        _prep_kernel,
        grid=(b, s // ts),
        in_specs=[zspec(MLA_Q_RANK, _Z_CQ), zspec(MLA_KV_RANK, _Z_CKV), zspec(LANES, _Z_KR),
                  zspec(GQA_HEADS * GQA_HEAD_DIM, _Z_QG), zspec(GQA_KV_HEADS * GQA_HEAD_DIM, _Z_KG),
                  zspec(GQA_KV_HEADS * GQA_HEAD_DIM, _Z_VG)]
                 + [full(g) for g in gs] + [full(wq), full(wkv)] + [tab_spec] * 4,
        out_specs=[ospec(qk), ospec(qk), ospec(2 * MLA_HEADS * MLA_V_DIM),
                   ospec(GQA_HEADS * GQA_HEAD_DIM), ospec(GQA_KV_HEADS * GQA_HEAD_DIM),
                   ospec(2 * GQA_KV_HEADS * GQA_HEAD_DIM)],
        out_shape=[jax.ShapeDtypeStruct((b, s, qk), BF16),
                   jax.ShapeDtypeStruct((b, s, qk), BF16),
                   jax.ShapeDtypeStruct((b, s, 2 * MLA_HEADS * MLA_V_DIM), BF16),
                   jax.ShapeDtypeStruct((b, s, GQA_HEADS * GQA_HEAD_DIM), BF16),
                   jax.ShapeDtypeStruct((b, s, GQA_KV_HEADS * GQA_HEAD_DIM), BF16),
                   jax.ShapeDtypeStruct((b, s, 2 * GQA_KV_HEADS * GQA_HEAD_DIM), BF16)],
        compiler_params=_params(("arbitrary", "arbitrary")),
        name="prep",
    )(z, z, z, z, z, z, *gs, wq, wkv, *tabs)


def _attn_kernel(q_ref, k_ref, v_ref, o_ref, *, heads, dk, dv, shared_kv):
    tk = k_ref.shape[1] // ATTN_KEY_CHUNKS
    for g in range(heads):
        kv = 0 if shared_kv else g
        q = q_ref[0, :, g * dk:(g + 1) * dk]
        m = acc = None
        for c in range(ATTN_KEY_CHUNKS):
            k = k_ref[0, c * tk:(c + 1) * tk, kv * dk:(kv + 1) * dk]
            v = v_ref[0, c * tk:(c + 1) * tk, 2 * kv * dv:2 * (kv + 1) * dv]
            s = lax.dot_general(q, k, (((1,), (1,)), ((), ())), preferred_element_type=F32)
            mc = jnp.max(s, axis=-1, keepdims=True)
            m_new = mc if m is None else jnp.maximum(m, mc)
            pv = jnp.dot(jnp.exp(s - m_new).astype(BF16), v, preferred_element_type=F32)
            acc = pv if acc is None else acc * jnp.exp(m - m_new) + pv
            m = m_new
        o_ref[0, :, g * dv:(g + 1) * dv] = (acc[:, :dv] / acc[:, dv:dv + 1]).astype(BF16)


def _attention(q, k, v, *, groups, heads, dk, dv, shared_kv, tq, name):
    b, s, _ = q.shape
    kvh = 1 if shared_kv else heads
    return pl.pallas_call(
        functools.partial(_attn_kernel, heads=heads, dk=dk, dv=dv, shared_kv=shared_kv),
        grid=(b, groups, s // tq),
        in_specs=[pl.BlockSpec((1, tq, heads * dk), lambda bi, g, i: (bi, i, g)),
                  pl.BlockSpec((1, s, kvh * dk), lambda bi, g, i: (bi, 0, g)),
                  pl.BlockSpec((1, s, kvh * 2 * dv), lambda bi, g, i: (bi, 0, g))],
        out_specs=pl.BlockSpec((1, tq, heads * dv), lambda bi, g, i: (bi, i, g)),
        out_shape=jax.ShapeDtypeStruct((b, s, groups * heads * dv), BF16),
        compiler_params=_params(("arbitrary", "arbitrary", "arbitrary")),
        name=name,
    )(q, k, v)


def _merge_kernel(oa_ref, ob_ref, ga_ref, gb_ref, x_ref, mod_ref, gffn_ref,
                  wa_ref, wb_ref, wo_ref, wr_ref,
                  x1_ref, h2_ref, lg_ref):
    a = jnp.dot(oa_ref[0], wa_ref[...], preferred_element_type=F32)
    bb = jnp.dot(ob_ref[0], wb_ref[...], preferred_element_type=F32)
    merged = (jax.nn.sigmoid(ga_ref[0].astype(F32)) * a
              + jax.nn.sigmoid(gb_ref[0].astype(F32)) * bb).astype(BF16)
    y = jnp.dot(merged, wo_ref[...], preferred_element_type=F32)
    x1 = x_ref[0] + mod_ref[0, 2:3, :] * y
    x1_ref[0] = x1
    h2 = _rms(x1) * gffn_ref[...] * (1.0 + mod_ref[0, 4:5, :]) + mod_ref[0, 3:4, :]
    hi = h2.astype(BF16)
    lo = (h2 - hi.astype(F32)).astype(BF16)
    h2_ref[0] = hi
    lg = jnp.dot(jnp.concatenate([hi, lo], axis=0), wr_ref[...], preferred_element_type=F32)
    tm = hi.shape[0]
    lg = lg[:tm] + lg[tm:]
    lg_ref[0] = lg[:, :LANES] + lg[:, LANES:]


def _merge(o_a, o_b, z, x, mod, g_ffn, wa, wb, wo, wr, tm):
    b, s, d = x.shape
    na, nb = o_a.shape[2], o_b.shape[2]
    tok = lambda width, col=0: pl.BlockSpec((1, tm, width), lambda bi, i: (bi, i, col // width))
    res = lambda a: _resident(a.shape, lambda bi, i: (0,) * a.ndim)
    return pl.pallas_call(
        _merge_kernel,
        grid=(b, s // tm),
        in_specs=[tok(na), tok(nb), tok(d, _Z_GATE_A), tok(d, _Z_GATE_B), tok(d),
                  pl.BlockSpec((1, 6, d), lambda bi, i: (bi, 0, 0)),
                  pl.BlockSpec((1, d), lambda bi, i: (0, 0)),
                  res(wa), res(wb), res(wo), res(wr)],
        out_specs=[tok(d), tok(d), tok(LANES)],
        out_shape=[jax.ShapeDtypeStruct((b, s, d), F32),
                   jax.ShapeDtypeStruct((b, s, d), BF16),
                   jax.ShapeDtypeStruct((b, s, LANES), F32)],
        compiler_params=_params(("arbitrary", "arbitrary")),
        name="merge",
    )(o_a, o_b, z, z, x, mod, g_ffn.reshape(1, d), wa, wb, wo, wr)


def _lane_cumsum(x):
    n = x.shape[1]
    lane = lax.broadcasted_iota(jnp.int32, x.shape, 1)
    d = 1
    while d < n:
        x = x + jnp.where(lane >= d, pltpu.roll(x, d, 1), 0.0)
        d *= 2
    return x


def _router_kernel(lg_ref, rank_ref, wval_ref, rankc_ref, *, cap):
    e = N_EXPERTS
    nb, s = lg_ref.shape[:2]
    affs = []
    for bi in range(nb):
        lt = jnp.transpose(lg_ref[bi])[:e, :]
        ex = jnp.exp(lt - jnp.max(lt, axis=0, keepdims=True))
        affs.append(ex / jnp.sum(ex, axis=0, keepdims=True))
    aff = jnp.concatenate(affs, axis=0)
    rows = nb * e

    def step(_, carry):
        lo, hi = carry
        mid = lo + ((hi - lo) >> 1)
        cnt = jnp.sum((aff >= pltpu.bitcast(mid, F32)).astype(jnp.int32), axis=1, keepdims=True)
        ok = cnt >= cap
        return jnp.where(ok, mid, lo), jnp.where(ok, hi, mid)

    lo0 = jnp.zeros((rows, 1), jnp.int32)
    hi0 = jnp.full((rows, 1), 0x3F800001, jnp.int32)
    thr_bits, _ = lax.fori_loop(0, 31, step, (lo0, hi0))
    thr = pltpu.bitcast(thr_bits, F32)

    gt = aff > thr
    eq = aff == thr
    need = (cap - jnp.sum(gt.astype(jnp.int32), axis=1, keepdims=True)).astype(F32)
    ceq = _lane_cumsum(eq.astype(F32))
    sel = gt | (eq & (ceq <= need))
    rank = jnp.where(sel, _lane_cumsum(sel.astype(F32)) - 1.0, -1.0)
    wval = jnp.where(sel, aff, 0.0)
    pad = jnp.full((LANES - e, s), -1.0, F32)
    for bi in range(nb):
        rank_b = rank[bi * e:(bi + 1) * e]
        rank_ref[bi] = rank_b
        wval_ref[bi] = wval[bi * e:(bi + 1) * e]
        rankc_ref[bi] = jnp.transpose(jnp.concatenate([rank_b, pad], axis=0))


def _router(logits, cap):
    b, s, _ = logits.shape
    e = N_EXPERTS
    whole = lambda *shape: pl.BlockSpec(shape, lambda i: (0,) * len(shape))
    return pl.pallas_call(
        functools.partial(_router_kernel, cap=cap),
        grid=(1,),
        in_specs=[whole(b, s, LANES)],
        out_specs=[whole(b, e, s), whole(b, e, s), whole(b, s, LANES)],
        out_shape=[jax.ShapeDtypeStruct((b, e, s), F32),
                   jax.ShapeDtypeStruct((b, e, s), F32),
                   jax.ShapeDtypeStruct((b, s, LANES), F32)],
        compiler_params=_params(("arbitrary",)),
        name="router",
    )(logits)


def _gather_kernel(rank_ref, wval_ref, h2_ref, xg_ref, val_ref, *, cap):
    s = rank_ref.shape[3]
    r_iota = lax.broadcasted_iota(jnp.int32, (cap, s), 0).astype(F32)
    for k in range(GATHER_EXPERTS_PER_STEP):
        hot = rank_ref[0, k] == r_iota
        xg_ref[0, k] = jnp.dot(hot.astype(BF16), h2_ref[0],
                               preferred_element_type=F32).astype(BF16)
        val_ref[0, k] = jnp.sum(jnp.where(hot, wval_ref[0, k], 0.0), axis=1, keepdims=True)


def _gather(rank, wval, h2, cap):
    b, e, s = rank.shape
    d = h2.shape[2]
    ge = GATHER_EXPERTS_PER_STEP
    assert e % ge == 0
    r4 = rank.reshape(b, e, 1, s)
    w4 = wval.reshape(b, e, 1, s)
    row = pl.BlockSpec((1, ge, 1, s), lambda bi, ei: (bi, ei, 0, 0))
    return pl.pallas_call(
        functools.partial(_gather_kernel, cap=cap),
        grid=(b, e // ge),
        in_specs=[row, row, pl.BlockSpec((1, s, d), lambda bi, ei: (bi, 0, 0))],
        out_specs=[pl.BlockSpec((1, ge, cap, d), lambda bi, ei: (bi, ei, 0, 0)),
                   pl.BlockSpec((1, ge, cap, 1), lambda bi, ei: (bi, ei, 0, 0))],
        out_shape=[jax.ShapeDtypeStruct((b, e, cap, d), BF16),
                   jax.ShapeDtypeStruct((b, e, cap, 1), F32)],
        compiler_params=_params(("arbitrary", "arbitrary")),
        name="gather",
    )(r4, w4, h2)


def _expert_kernel(xg_ref, val_ref, wg_ref, wu_ref, wd_ref, y_ref, wg_s, wu_s, wd_s, *, n_exp):
    p = pl.program_id(0)
    bi = pl.program_id(1)
    stage = p % 2
    ready = 1 - stage

    def stage_chunk():
        for w_ref, w_s in ((wg_ref, wg_s), (wu_ref, wu_s), (wd_ref, wd_s)):
            rows = w_ref.shape[1]
            w_s[stage, pl.ds(pl.multiple_of(bi * rows, rows), rows), :] = w_ref[0].astype(BF16)

    def run_expert():
        nb, cap = xg_ref.shape[0], xg_ref.shape[2]
        xg = jnp.concatenate([xg_ref[k, 0] for k in range(nb)], axis=0)
        a = jnp.dot(xg, wg_s[ready], preferred_element_type=F32)
        u = jnp.dot(xg, wu_s[ready], preferred_element_type=F32)
        hm = (a * jax.nn.sigmoid(a) * u).astype(BF16)
        y = jnp.dot(hm, wd_s[ready], preferred_element_type=F32)
        for k in range(nb):
            y_ref[k, 0] = (y[k * cap:(k + 1) * cap] * val_ref[k, 0]).astype(BF16)

    pl.when(p == 0)(stage_chunk)

    @pl.when((p > 0) & (p < n_exp))
    def _():
        stage_chunk()
        run_expert()

    pl.when(p == n_exp)(run_expert)


def _experts(xg, val, wg, wu, wd):
    b, e, cap, d = xg.shape
    f = wg.shape[2]
    nbs = EXPERT_BATCHES_PER_STEP
    steps = b // nbs
    assert b % nbs == 0 and d % steps == 0 and f % steps == 0
    last = e - 1
    tok = lambda p, bi: (jnp.where(p == 0, 0, bi), jnp.maximum(p - 1, 0), 0, 0)
    chunk = lambda p, bi: (jnp.minimum(p, last), jnp.where(p <= last, bi, steps - 1), 0)
    return pl.pallas_call(
        functools.partial(_expert_kernel, n_exp=e),
        grid=(e + 1, steps),
        in_specs=[pl.BlockSpec((nbs, 1, cap, d), tok),
                  pl.BlockSpec((nbs, 1, cap, 1), tok),
                  pl.BlockSpec((1, d // steps, f), chunk),
                  pl.BlockSpec((1, d // steps, f), chunk),
                  pl.BlockSpec((1, f // steps, d), chunk)],
        out_specs=pl.BlockSpec((nbs, 1, cap, d), tok),
        out_shape=jax.ShapeDtypeStruct((b, e, cap, d), BF16),
        scratch_shapes=[pltpu.VMEM((2, d, f), BF16),
                        pltpu.VMEM((2, d, f), BF16),
                        pltpu.VMEM((2, f, d), BF16)],
        compiler_params=_params(("arbitrary", "arbitrary")),
        name="experts",
    )(xg, val, wg, wu, wd)


def _combine_kernel(rankc_ref, y_ref, x1_ref, mod_ref, gfin_ref, o_ref, *, cap, final):
    tm = rankc_ref.shape[1]
    rc = rankc_ref[0]
    col = lax.broadcasted_iota(jnp.int32, (tm, cap), 1).astype(F32)
    hot = jnp.concatenate(
        [(rc[:, e:e + 1] == col).astype(BF16) for e in range(N_EXPERTS)], axis=1)
    moe = jnp.dot(hot, y_ref[0], preferred_element_type=F32)
    x2 = x1_ref[0] + mod_ref[0, 5:6, :] * moe
    if final:
        x2 = _rms(x2) * gfin_ref[...]
    o_ref[0] = x2


def _combine(rankc, y, x1, mod, g_final, cap, tm, final):
    b, s, d = x1.shape
    y2 = y.reshape(b, N_EXPERTS * cap, d)
    tok = lambda width: pl.BlockSpec((1, tm, width), lambda bi, i: (bi, i, 0))
    return pl.pallas_call(
        functools.partial(_combine_kernel, cap=cap, final=final),
        grid=(b, s // tm),
        in_specs=[tok(LANES),
                  pl.BlockSpec((1, N_EXPERTS * cap, d), lambda bi, i: (bi, 0, 0)),
                  tok(d),
                  pl.BlockSpec((1, 6, d), lambda bi, i: (bi, 0, 0)),
                  pl.BlockSpec((1, d), lambda bi, i: (0, 0))],
        out_specs=tok(d),
        out_shape=jax.ShapeDtypeStruct((b, s, d), F32),
        compiler_params=_params(("arbitrary", "arbitrary")),
        name="combine",
    )(rankc, y2, x1, mod, g_final.reshape(1, d))


def _rope_tables(s):
    half = MLA_ROPE_DIM // 2
    freqs = ROPE_THETA ** (-jnp.arange(half, dtype=F32) / half)
    t = jnp.arange(s)
    ang = lambda pos: pos.astype(F32)[:, None] * freqs[None, :]
    a1, ar, ac = ang(t), ang(t // GRID_W), ang(t % GRID_W)
    z = jnp.zeros((s, half), F32)
    cat = lambda *p: jnp.concatenate(p, axis=1)
    c1 = cat(jnp.cos(a1), z, jnp.cos(a1), z)
    s1 = cat(-jnp.sin(a1), z, jnp.sin(a1), z)
    c2 = cat(jnp.cos(ar), jnp.cos(ac), jnp.cos(ar), jnp.cos(ac))
    s2 = cat(-jnp.sin(ar), -jnp.sin(ac), jnp.sin(ar), jnp.sin(ac))
    return c1, s1, c2, s2


def _pair_apart(w, heads):
    lead = w.shape[:-1]
    w = w.reshape(*lead, heads, 4, GQA_HEAD_DIM // 4)
    return w[..., jnp.array([0, 2, 1, 3]), :].reshape(*lead, heads * GQA_HEAD_DIM)


def _spread_rope(w):
    half = MLA_ROPE_DIM // 2
    z = jnp.zeros(w.shape[:-1] + (half,), w.dtype)
    return jnp.concatenate([w[..., :half], z, w[..., half:], z], axis=-1)


def _regroup_kernel(w_ref, o_ref, *, segments):
    quarter = GQA_HEAD_DIM // 4
    out = 0

    def move(src, rows):
        nonlocal out
        o_ref[out:out + rows, :] = w_ref[src:src + rows, :].astype(BF16)
        out += rows

    def zeros(rows):
        nonlocal out
        o_ref[out:out + rows, :] = jnp.zeros((rows, o_ref.shape[1]), BF16)
        out += rows

    for src, rows, kind in segments:
        if kind == "copy":
            move(src, rows)
        elif kind == "zero":
            zeros(rows)
        elif kind == "pair_apart":
            for head in range(0, rows, GQA_HEAD_DIM):
                for block in (0, 2, 1, 3):
                    move(src + head + block * quarter, quarter)
        else:
            half = rows // 2
            for part in range(2):
                move(src + part * half, half)
                zeros(half)
    assert out == o_ref.shape[0]


def _regroup_w_in(w_in):
    d, n = w_in.shape
    names = ("cq", "ckv", "kr", "qg", "kg", "vg", "ga", "gb")
    widths = (MLA_Q_RANK, MLA_KV_RANK, MLA_ROPE_DIM, GQA_HEADS * GQA_HEAD_DIM,
              GQA_KV_HEADS * GQA_HEAD_DIM, GQA_KV_HEADS * GQA_HEAD_DIM, d, d)
    src, off = {}, 0
    for name, w in zip(names, widths):
        src[name] = (off, w)
        off += w
    assert off == n
    segments = [src["ga"] + ("copy",), src["gb"] + ("copy",), src["qg"] + ("pair_apart",),
                src["kg"] + ("pair_apart",), src["vg"] + ("copy",), src["cq"] + ("copy",),
                src["ckv"] + ("copy",), src["kr"] + ("spread",)]
    used = sum(2 * w if kind == "spread" else w for _, w, kind in segments)
    segments.append((0, _Z_COLS - used, "zero"))
    cols = 256
    return pl.pallas_call(
        functools.partial(_regroup_kernel, segments=tuple(segments)),
        grid=(d // cols,),
        in_specs=[pl.BlockSpec((n, cols), lambda i: (0, i))],
        out_specs=pl.BlockSpec((_Z_COLS, cols), lambda i: (0, i)),
        out_shape=jax.ShapeDtypeStruct((_Z_COLS, d), BF16),
        compiler_params=_params(("arbitrary",)),
        name="regroup",
    )(jnp.swapaxes(w_in, 0, 1))


def _regroup_w_uq(w_uq):
    r = w_uq.shape[0]
    w = w_uq.astype(BF16).reshape(r, MLA_HEADS, MLA_NOPE_DIM + MLA_ROPE_DIM)
    w = jnp.concatenate([w[..., :MLA_NOPE_DIM], _spread_rope(w[..., MLA_NOPE_DIM:])], axis=-1)
    return w.reshape(r, MLA_HEADS * MLA_QK_PAD)


def _regroup_w_ukv(w_ukv):
    r = w_ukv.shape[0]
    w = w_ukv.reshape(r, MLA_HEADS, MLA_NOPE_DIM + MLA_V_DIM)
    wk = w[:, :, :MLA_NOPE_DIM].reshape(r, MLA_HEADS * MLA_NOPE_DIM)
    wv = w[:, :, MLA_NOPE_DIM:].reshape(r, MLA_HEADS * MLA_V_DIM)
    return jnp.concatenate([wk, wv], axis=1).astype(BF16)


def _split_router(w_router):
    d, e = w_router.shape
    w = jnp.pad(w_router, ((0, 0), (0, LANES - e)))
    hi = w.astype(BF16)
    lo = (w - hi.astype(F32)).astype(BF16)
    return jnp.concatenate([hi, lo], axis=1)


def kernel(x, c, w_ada, b_ada, g_mix, w_in, g_cq, g_ckv, w_uq, w_ukv, g_qn, g_kn, w_br_a, w_br_b, w_out, g_ffn, w_router, w_e_gate, w_e_up, w_e_down, g_final):
    b, s, d = x.shape
    depth = w_ada.shape[0]
    cap = EC_CAPACITY * s // N_EXPERTS
    assert d == _Z_GATE_B and s % LANES == 0 and cap % LANES == 0
    rows = lambda n: min(n, s)
    tabs = _rope_tables(s)
    for l in range(depth):
        mod = _adaln(c, w_ada[l], b_ada[l]).reshape(b, 6, d)
        z = _inproj(x, mod, g_mix[l], _regroup_w_in(w_in[l]), rows(INPROJ_ROWS))
        qm, km, vm, qb, kb, vb = _prep(z, g_cq[l], g_ckv[l], _pair_apart(g_qn[l], 1),
                                       _pair_apart(g_kn[l], 1), _regroup_w_uq(w_uq[l]),
                                       _regroup_w_ukv(w_ukv[l]), tabs, rows(PREP_ROWS))
        o_a = _attention(qm, km, vm, groups=MLA_HEADS // MLA_HEADS_PER_STEP,
                         heads=MLA_HEADS_PER_STEP, dk=MLA_QK_PAD, dv=MLA_V_DIM, shared_kv=False,
                         tq=rows(ATTN_Q_ROWS), name="mla_attn")
        o_b = _attention(qb, kb, vb, groups=GQA_KV_HEADS, heads=GQA_GROUP, dk=GQA_HEAD_DIM,
                         dv=GQA_HEAD_DIM, shared_kv=True, tq=rows(ATTN_Q_ROWS), name="gqa_attn")
        wr = _split_router(w_router[l])
        x1, h2, logits = _merge(o_a, o_b, z, x, mod, g_ffn[l], w_br_a[l].astype(BF16),
                                w_br_b[l].astype(BF16), w_out[l].astype(BF16), wr,
                                rows(MERGE_ROWS))
        rank, wval, rankc = _router(logits, cap)
        xg, val = _gather(rank, wval, h2, cap)
        y = _experts(xg, val, w_e_gate[l], w_e_up[l], w_e_down[l])
        x = _combine(rankc, y, x1, mod, g_final, cap, rows(COMBINE_ROWS),
                     final=(l == depth - 1))
    return x
```

```python
import functools

import jax
import jax.numpy as jnp
from jax import lax
from jax.experimental import pallas as pl
from jax.experimental.pallas import tpu as pltpu

F32 = jnp.float32
BF16 = jnp.bfloat16

GRID_W = 64
ROPE_THETA = 10000.0
RMS_EPS = 1e-6

MLA_HEADS = 8
MLA_Q_RANK = 512
MLA_KV_RANK = 256
MLA_NOPE_DIM = 128
MLA_ROPE_DIM = 64
MLA_V_DIM = 128
MLA_QK_PAD = 256
MLA_HEADS_PER_STEP = 4
ATTN_KEY_CHUNKS = 4

GQA_HEADS = 8
GQA_KV_HEADS = 2
GQA_HEAD_DIM = 128
GQA_GROUP = GQA_HEADS // GQA_KV_HEADS

N_EXPERTS = 16
EC_CAPACITY = 2

LANES = 128

ADALN_COLS = 2048

INPROJ_ROWS = 512
NORM_CHUNK = 128
PREP_ROWS = 1024
ATTN_Q_ROWS = 1024
GATHER_EXPERTS_PER_STEP = 8
EXPERT_BATCHES_PER_STEP = 2
MERGE_ROWS = 256
COMBINE_ROWS = 256
V7X_VMEM_LIMIT = 56 * 1024 * 1024

_Z_GATE_A = 0
_Z_GATE_B = 2048
_Z_QG = 4096
_Z_KG = 5120
_Z_VG = 5376
_Z_CQ = 5632
_Z_CKV = 6144
_Z_KR = 6400
_Z_COLS = 6656
_Z_TILE = _Z_COLS // 2


def _params(sem, vmem=V7X_VMEM_LIMIT):
    return pltpu.CompilerParams(dimension_semantics=sem, vmem_limit_bytes=vmem)


def _resident(shape, index_map):
    return pl.BlockSpec(shape, index_map, pipeline_mode=pl.Buffered(1))


def _rms(x):
    return x * lax.rsqrt(jnp.mean(x * x, axis=-1, keepdims=True) + RMS_EPS)


def _rope128(x, cos, sin):
    return x * cos + pltpu.roll(x, LANES // 2, 1) * sin


def _adaln_kernel(c_ref, w_ref, b_ref, o_ref):
    c = c_ref[...]
    sc = (c * jax.nn.sigmoid(c)).astype(BF16)
    o_ref[...] = jnp.dot(sc, w_ref[...].astype(BF16), preferred_element_type=F32) + b_ref[...]


def _adaln(c, w_ada, b_ada):
    b, d = c.shape
    n = w_ada.shape[1]
    tn = ADALN_COLS
    return pl.pallas_call(
        _adaln_kernel,
        grid=(n // tn,),
        in_specs=[pl.BlockSpec((b, d), lambda j: (0, 0)),
                  pl.BlockSpec((d, tn), lambda j: (0, j)),
                  pl.BlockSpec((1, tn), lambda j: (0, j))],
        out_specs=pl.BlockSpec((b, tn), lambda j: (0, j)),
        out_shape=jax.ShapeDtypeStruct((b, n), F32),
        compiler_params=_params(("arbitrary",)),
        name="adaln",
    )(c, w_ada, b_ada.reshape(1, n))


def _inproj_kernel(x_ref, mod_ref, g_ref, w_ref, z_ref, h_scr):
    @pl.when(pl.program_id(2) == 0)
    def _():
        gain = g_ref[...] * (1.0 + mod_ref[0, 1:2, :])
        shift = mod_ref[0, 0:1, :]

        def body(i, carry):
            r = pl.ds(pl.multiple_of(i * NORM_CHUNK, NORM_CHUNK), NORM_CHUNK)
            h_scr[r, :] = (_rms(x_ref[0, r, :]) * gain + shift).astype(BF16)
            return carry

        lax.fori_loop(0, h_scr.shape[0] // NORM_CHUNK, body, 0)

    z_ref[0] = lax.dot_general(h_scr[...], w_ref[...], (((1,), (1,)), ((), ())),
                               preferred_element_type=F32).astype(BF16)


def _inproj(x, mod, g_mix, w_z, tm):
    b, s, d = x.shape
    nj = _Z_COLS // _Z_TILE
    return pl.pallas_call(
        _inproj_kernel,
        grid=(b, s // tm, nj),
        in_specs=[pl.BlockSpec((1, tm, d), lambda bi, i, j: (bi, i, 0)),
                  pl.BlockSpec((1, 6, d), lambda bi, i, j: (bi, 0, 0)),
                  pl.BlockSpec((1, d), lambda bi, i, j: (0, 0)),
                  pl.BlockSpec((_Z_TILE, d), lambda bi, i, j: (j, 0))],
        out_specs=pl.BlockSpec((1, tm, _Z_TILE), lambda bi, i, j: (bi, i, j)),
        out_shape=jax.ShapeDtypeStruct((b, s, _Z_COLS), BF16),
        scratch_shapes=[pltpu.VMEM((tm, d), BF16)],
        compiler_params=_params(("arbitrary", "arbitrary", "arbitrary")),
        name="inproj",
    )(x, mod, g_mix.reshape(1, d), w_z)


def _prep_kernel(cq_ref, ckv_ref, kr_ref, qg_ref, kg_ref, vg_ref,
                 gcq_ref, gckv_ref, gqn_ref, gkn_ref, wq_ref, wkv_ref,
                 c1_ref, s1_ref, c2_ref, s2_ref,
                 qm_ref, km_ref, vm_ref, qb_ref, kb_ref, vb_ref):
    c1, s1 = c1_ref[...], s1_ref[...]
    c2, s2 = c2_ref[...], s2_ref[...]
    ts = cq_ref.shape[1]
    one_col = (lax.broadcasted_iota(jnp.int32, (ts, LANES), 1) == 0).astype(BF16)

    cqn = (_rms(cq_ref[0].astype(F32)) * gcq_ref[...]).astype(BF16)
    q = jnp.dot(cqn, wq_ref[...], preferred_element_type=F32)
    q = q * ((MLA_NOPE_DIM + MLA_ROPE_DIM) ** -0.5)
    for h in range(MLA_HEADS):
        lo = h * MLA_QK_PAD
        qm_ref[0, :, lo:lo + LANES] = q[:, lo:lo + LANES].astype(BF16)
        qm_ref[0, :, lo + LANES:lo + 2 * LANES] = _rope128(
            q[:, lo + LANES:lo + 2 * LANES], c1, s1).astype(BF16)

    ckvn = (_rms(ckv_ref[0].astype(F32)) * gckv_ref[...]).astype(BF16)
    kv = jnp.dot(ckvn, wkv_ref[...], preferred_element_type=F32)
    krr = _rope128(kr_ref[0].astype(F32), c1, s1).astype(BF16)
    nk = MLA_HEADS * MLA_NOPE_DIM
    for h in range(MLA_HEADS):
        lo = h * MLA_QK_PAD
        km_ref[0, :, lo:lo + LANES] = kv[:, h * LANES:(h + 1) * LANES].astype(BF16)
        km_ref[0, :, lo + LANES:lo + 2 * LANES] = krr
    for h in range(MLA_HEADS):
        lo = 2 * h * MLA_V_DIM
        vm_ref[0, :, lo:lo + MLA_V_DIM] = kv[:, nk + h * MLA_V_DIM:nk + (h + 1) * MLA_V_DIM].astype(BF16)
        vm_ref[0, :, lo + MLA_V_DIM:lo + 2 * MLA_V_DIM] = one_col

    gqn = gqn_ref[...] * (GQA_HEAD_DIM ** -0.5)
    for h in range(GQA_HEADS):
        lo = h * GQA_HEAD_DIM
        t = _rms(qg_ref[0, :, lo:lo + GQA_HEAD_DIM].astype(F32)) * gqn
        qb_ref[0, :, lo:lo + GQA_HEAD_DIM] = _rope128(t, c2, s2).astype(BF16)
    for h in range(GQA_KV_HEADS):
        lo = h * GQA_HEAD_DIM
        t = _rms(kg_ref[0, :, lo:lo + GQA_HEAD_DIM].astype(F32)) * gkn_ref[...]
        kb_ref[0, :, lo:lo + GQA_HEAD_DIM] = _rope128(t, c2, s2).astype(BF16)
        vb_ref[0, :, 2 * lo:2 * lo + GQA_HEAD_DIM] = vg_ref[0, :, lo:lo + GQA_HEAD_DIM]
        vb_ref[0, :, 2 * lo + GQA_HEAD_DIM:2 * lo + 2 * GQA_HEAD_DIM] = one_col


def _prep(z, g_cq, g_ckv, g_qn, g_kn, wq, wkv, tabs, ts):
    b, s, _ = z.shape

    def zspec(width, col):
        return pl.BlockSpec((1, ts, width), lambda bi, i: (bi, i, col // width))

    def full(a):
        return pl.BlockSpec(a.shape, lambda bi, i: (0,) * a.ndim)

    tab_spec = pl.BlockSpec((ts, LANES), lambda bi, i: (i, 0))
    row = lambda g: g.reshape(1, -1)
    gs = [row(g_cq), row(g_ckv), row(g_qn), row(g_kn)]
    qk = MLA_HEADS * MLA_QK_PAD

    def ospec(width):
        return pl.BlockSpec((1, ts, width), lambda bi, i: (bi, i, 0))

    return pl.pallas_call(
        _prep_kernel,
        grid=(b, s // ts),
        in_specs=[zspec(MLA_Q_RANK, _Z_CQ), zspec(MLA_KV_RANK, _Z_CKV), zspec(LANES, _Z_KR),
                  zspec(GQA_HEADS * GQA_HEAD_DIM, _Z_QG), zspec(GQA_KV_HEADS * GQA_HEAD_DIM, _Z_KG),
                  zspec(GQA_KV_HEADS * GQA_HEAD_DIM, _Z_VG)]
                 + [full(g) for g in gs] + [full(wq), full(wkv)] + [tab_spec] * 4,
        out_specs=[ospec(qk), ospec(qk), ospec(2 * MLA_HEADS * MLA_V_DIM),
                   ospec(GQA_HEADS * GQA_HEAD_DIM), ospec(GQA_KV_HEADS * GQA_HEAD_DIM),
                   ospec(2 * GQA_KV_HEADS * GQA_HEAD_DIM)],
        out_shape=[jax.ShapeDtypeStruct((b, s, qk), BF16),
                   jax.ShapeDtypeStruct((b, s, qk), BF16),
                   jax.ShapeDtypeStruct((b, s, 2 * MLA_HEADS * MLA_V_DIM), BF16),
                   jax.ShapeDtypeStruct((b, s, GQA_HEADS * GQA_HEAD_DIM), BF16),
                   jax.ShapeDtypeStruct((b, s, GQA_KV_HEADS * GQA_HEAD_DIM), BF16),
                   jax.ShapeDtypeStruct((b, s, 2 * GQA_KV_HEADS * GQA_HEAD_DIM), BF16)],
        compiler_params=_params(("arbitrary", "arbitrary")),
        name="prep",
    )(z, z, z, z, z, z, *gs, wq, wkv, *tabs)


def _attn_kernel(q_ref, k_ref, v_ref, o_ref, *, heads, dk, dv, shared_kv):
    tk = k_ref.shape[1] // ATTN_KEY_CHUNKS
    for g in range(heads):
        kv = 0 if shared_kv else g
        q = q_ref[0, :, g * dk:(g + 1) * dk]
        m = acc = None
        for c in range(ATTN_KEY_CHUNKS):
            k = k_ref[0, c * tk:(c + 1) * tk, kv * dk:(kv + 1) * dk]
            v = v_ref[0, c * tk:(c + 1) * tk, 2 * kv * dv:2 * (kv + 1) * dv]
            s = lax.dot_general(q, k, (((1,), (1,)), ((), ())), preferred_element_type=F32)
            mc = jnp.max(s, axis=-1, keepdims=True)
            m_new = mc if m is None else jnp.maximum(m, mc)
            pv = jnp.dot(jnp.exp(s - m_new).astype(BF16), v, preferred_element_type=F32)
            acc = pv if acc is None else acc * jnp.exp(m - m_new) + pv
            m = m_new
        o_ref[0, :, g * dv:(g + 1) * dv] = (acc[:, :dv] / acc[:, dv:dv + 1]).astype(BF16)


def _attention(q, k, v, *, groups, heads, dk, dv, shared_kv, tq, name):
    b, s, _ = q.shape
    kvh = 1 if shared_kv else heads
    return pl.pallas_call(
        functools.partial(_attn_kernel, heads=heads, dk=dk, dv=dv, shared_kv=shared_kv),
        grid=(b, groups, s // tq),
        in_specs=[pl.BlockSpec((1, tq, heads * dk), lambda bi, g, i: (bi, i, g)),
                  pl.BlockSpec((1, s, kvh * dk), lambda bi, g, i: (bi, 0, g)),
                  pl.BlockSpec((1, s, kvh * 2 * dv), lambda bi, g, i: (bi, 0, g))],
        out_specs=pl.BlockSpec((1, tq, heads * dv), lambda bi, g, i: (bi, i, g)),
        out_shape=jax.ShapeDtypeStruct((b, s, groups * heads * dv), BF16),
        compiler_params=_params(("arbitrary", "arbitrary", "arbitrary")),
        name=name,
    )(q, k, v)


def _merge_kernel(oa_ref, ob_ref, ga_ref, gb_ref, x_ref, mod_ref, gffn_ref,
                  wa_ref, wb_ref, wo_ref, wr_ref,
                  x1_ref, h2_ref, lg_ref):
    a = jnp.dot(oa_ref[0], wa_ref[...], preferred_element_type=F32)
    bb = jnp.dot(ob_ref[0], wb_ref[...], preferred_element_type=F32)
    merged = (jax.nn.sigmoid(ga_ref[0].astype(F32)) * a
              + jax.nn.sigmoid(gb_ref[0].astype(F32)) * bb).astype(BF16)
    y = jnp.dot(merged, wo_ref[...], preferred_element_type=F32)
    x1 = x_ref[0] + mod_ref[0, 2:3, :] * y
    x1_ref[0] = x1
    h2 = _rms(x1) * gffn_ref[...] * (1.0 + mod_ref[0, 4:5, :]) + mod_ref[0, 3:4, :]
    hi = h2.astype(BF16)
    lo = (h2 - hi.astype(F32)).astype(BF16)
    h2_ref[0] = hi
    lg = jnp.dot(jnp.concatenate([hi, lo], axis=0), wr_ref[...], preferred_element_type=F32)
    tm = hi.shape[0]
    lg = lg[:tm] + lg[tm:]
    lg_ref[0] = lg[:, :LANES] + lg[:, LANES:]


def _merge(o_a, o_b, z, x, mod, g_ffn, wa, wb, wo, wr, tm):
    b, s, d = x.shape
    na, nb = o_a.shape[2], o_b.shape[2]
    tok = lambda width, col=0: pl.BlockSpec((1, tm, width), lambda bi, i: (bi, i, col // width))
    res = lambda a: _resident(a.shape, lambda bi, i: (0,) * a.ndim)
    return pl.pallas_call(
        _merge_kernel,
        grid=(b, s // tm),
        in_specs=[tok(na), tok(nb), tok(d, _Z_GATE_A), tok(d, _Z_GATE_B), tok(d),
                  pl.BlockSpec((1, 6, d), lambda bi, i: (bi, 0, 0)),
                  pl.BlockSpec((1, d), lambda bi, i: (0, 0)),
                  res(wa), res(wb), res(wo), res(wr)],
        out_specs=[tok(d), tok(d), tok(LANES)],
        out_shape=[jax.ShapeDtypeStruct((b, s, d), F32),
                   jax.ShapeDtypeStruct((b, s, d), BF16),
                   jax.ShapeDtypeStruct((b, s, LANES), F32)],
        compiler_params=_params(("arbitrary", "arbitrary")),
        name="merge",
    )(o_a, o_b, z, z, x, mod, g_ffn.reshape(1, d), wa, wb, wo, wr)


def _lane_cumsum(x):
    n = x.shape[1]
    lane = lax.broadcasted_iota(jnp.int32, x.shape, 1)
    d = 1
    while d < n:
        x = x + jnp.where(lane >= d, pltpu.roll(x, d, 1), 0.0)
        d *= 2
    return x


def _router_kernel(lg_ref, rank_ref, wval_ref, rankc_ref, *, cap):
    e = N_EXPERTS
    nb, s = lg_ref.shape[:2]
    affs = []
    for bi in range(nb):
        lt = jnp.transpose(lg_ref[bi])[:e, :]
        ex = jnp.exp(lt - jnp.max(lt, axis=0, keepdims=True))
        affs.append(ex / jnp.sum(ex, axis=0, keepdims=True))
    aff = jnp.concatenate(affs, axis=0)
    rows = nb * e

    def step(_, carry):
        lo, hi = carry
        mid = lo + ((hi - lo) >> 1)
        cnt = jnp.sum((aff >= pltpu.bitcast(mid, F32)).astype(jnp.int32), axis=1, keepdims=True)
        ok = cnt >= cap
        return jnp.where(ok, mid, lo), jnp.where(ok, hi, mid)

    lo0 = jnp.zeros((rows, 1), jnp.int32)
    hi0 = jnp.full((rows, 1), 0x3F800001, jnp.int32)
    thr_bits, _ = lax.fori_loop(0, 31, step, (lo0, hi0))
    thr = pltpu.bitcast(thr_bits, F32)

    gt = aff > thr
    eq = aff == thr
    need = (cap - jnp.sum(gt.astype(jnp.int32), axis=1, keepdims=True)).astype(F32)
    ceq = _lane_cumsum(eq.astype(F32))
    sel = gt | (eq & (ceq <= need))
    rank = jnp.where(sel, _lane_cumsum(sel.astype(F32)) - 1.0, -1.0)
    wval = jnp.where(sel, aff, 0.0)
    pad = jnp.full((LANES - e, s), -1.0, F32)
    for bi in range(nb):
        rank_b = rank[bi * e:(bi + 1) * e]
        rank_ref[bi] = rank_b
        wval_ref[bi] = wval[bi * e:(bi + 1) * e]
        rankc_ref[bi] = jnp.transpose(jnp.concatenate([rank_b, pad], axis=0))


def _router(logits, cap):
    b, s, _ = logits.shape
    e = N_EXPERTS
    whole = lambda *shape: pl.BlockSpec(shape, lambda i: (0,) * len(shape))
    return pl.pallas_call(
        functools.partial(_router_kernel, cap=cap),
        grid=(1,),
        in_specs=[whole(b, s, LANES)],
        out_specs=[whole(b, e, s), whole(b, e, s), whole(b, s, LANES)],
        out_shape=[jax.ShapeDtypeStruct((b, e, s), F32),
                   jax.ShapeDtypeStruct((b, e, s), F32),
                   jax.ShapeDtypeStruct((b, s, LANES), F32)],
        compiler_params=_params(("arbitrary",)),
        name="router",
    )(logits)


def _gather_kernel(rank_ref, wval_ref, h2_ref, xg_ref, val_ref, *, cap):
    s = rank_ref.shape[3]
    r_iota = lax.broadcasted_iota(jnp.int32, (cap, s), 0).astype(F32)
    for k in range(GATHER_EXPERTS_PER_STEP):
        hot = rank_ref[0, k] == r_iota
        xg_ref[0, k] = jnp.dot(hot.astype(BF16), h2_ref[0],
                               preferred_element_type=F32).astype(BF16)
        val_ref[0, k] = jnp.sum(jnp.where(hot, wval_ref[0, k], 0.0), axis=1, keepdims=True)


def _gather(rank, wval, h2, cap):
    b, e, s = rank.shape
    d = h2.shape[2]
    ge = GATHER_EXPERTS_PER_STEP
    assert e % ge == 0
    r4 = rank.reshape(b, e, 1, s)
    w4 = wval.reshape(b, e, 1, s)
    row = pl.BlockSpec((1, ge, 1, s), lambda bi, ei: (bi, ei, 0, 0))
    return pl.pallas_call(
        functools.partial(_gather_kernel, cap=cap),
        grid=(b, e // ge),
        in_specs=[row, row, pl.BlockSpec((1, s, d), lambda bi, ei: (bi, 0, 0))],
        out_specs=[pl.BlockSpec((1, ge, cap, d), lambda bi, ei: (bi, ei, 0, 0)),
                   pl.BlockSpec((1, ge, cap, 1), lambda bi, ei: (bi, ei, 0, 0))],
        out_shape=[jax.ShapeDtypeStruct((b, e, cap, d), BF16),
                   jax.ShapeDtypeStruct((b, e, cap, 1), F32)],
        compiler_params=_params(("arbitrary", "arbitrary")),
        name="gather",
    )(r4, w4, h2)


def _expert_kernel(xg_ref, val_ref, wg_ref, wu_ref, wd_ref, y_ref, wg_s, wu_s, wd_s, *, n_exp):
    p = pl.program_id(0)
    bi = pl.program_id(1)
    stage = p % 2
    ready = 1 - stage

    def stage_chunk():
        for w_ref, w_s in ((wg_ref, wg_s), (wu_ref, wu_s), (wd_ref, wd_s)):
            rows = w_ref.shape[1]
            w_s[stage, pl.ds(pl.multiple_of(bi * rows, rows), rows), :] = w_ref[0].astype(BF16)

    def run_expert():
        nb, cap = xg_ref.shape[0], xg_ref.shape[2]
        xg = jnp.concatenate([xg_ref[k, 0] for k in range(nb)], axis=0)
        a = jnp.dot(xg, wg_s[ready], preferred_element_type=F32)
        u = jnp.dot(xg, wu_s[ready], preferred_element_type=F32)
        hm = (a * jax.nn.sigmoid(a) * u).astype(BF16)
        y = jnp.dot(hm, wd_s[ready], preferred_element_type=F32)
        for k in range(nb):
            y_ref[k, 0] = (y[k * cap:(k + 1) * cap] * val_ref[k, 0]).astype(BF16)

    pl.when(p == 0)(stage_chunk)

    @pl.when((p > 0) & (p < n_exp))
    def _():
        stage_chunk()
        run_expert()

    pl.when(p == n_exp)(run_expert)


def _experts(xg, val, wg, wu, wd):
    b, e, cap, d = xg.shape
    f = wg.shape[2]
    nbs = EXPERT_BATCHES_PER_STEP
    steps = b // nbs
    assert b % nbs == 0 and d % steps == 0 and f % steps == 0
    last = e - 1
    tok = lambda p, bi: (jnp.where(p == 0, 0, bi), jnp.maximum(p - 1, 0), 0, 0)
    chunk = lambda p, bi: (jnp.minimum(p, last), jnp.where(p <= last, bi, steps - 1), 0)
    return pl.pallas_call(
        functools.partial(_expert_kernel, n_exp=e),
        grid=(e + 1, steps),
        in_specs=[pl.BlockSpec((nbs, 1, cap, d), tok),
                  pl.BlockSpec((nbs, 1, cap, 1), tok),
                  pl.BlockSpec((1, d // steps, f), chunk),
                  pl.BlockSpec((1, d // steps, f), chunk),
                  pl.BlockSpec((1, f // steps, d), chunk)],
        out_specs=pl.BlockSpec((nbs, 1, cap, d), tok),
        out_shape=jax.ShapeDtypeStruct((b, e, cap, d), BF16),
        scratch_shapes=[pltpu.VMEM((2, d, f), BF16),
                        pltpu.VMEM((2, d, f), BF16),
                        pltpu.VMEM((2, f, d), BF16)],
        compiler_params=_params(("arbitrary", "arbitrary")),
        name="experts",
    )(xg, val, wg, wu, wd)


def _combine_kernel(rankc_ref, y_ref, x1_ref, mod_ref, gfin_ref, o_ref, *, cap, final):
    tm = rankc_ref.shape[1]
    rc = rankc_ref[0]
    col = lax.broadcasted_iota(jnp.int32, (tm, cap), 1).astype(F32)
    hot = jnp.concatenate(
        [(rc[:, e:e + 1] == col).astype(BF16) for e in range(N_EXPERTS)], axis=1)
    moe = jnp.dot(hot, y_ref[0], preferred_element_type=F32)
    x2 = x1_ref[0] + mod_ref[0, 5:6, :] * moe
    if final:
        x2 = _rms(x2) * gfin_ref[...]
    o_ref[0] = x2


def _combine(rankc, y, x1, mod, g_final, cap, tm, final):
    b, s, d = x1.shape
    y2 = y.reshape(b, N_EXPERTS * cap, d)
    tok = lambda width: pl.BlockSpec((1, tm, width), lambda bi, i: (bi, i, 0))
    return pl.pallas_call(
        functools.partial(_combine_kernel, cap=cap, final=final),
        grid=(b, s // tm),
        in_specs=[tok(LANES),
                  pl.BlockSpec((1, N_EXPERTS * cap, d), lambda bi, i: (bi, 0, 0)),
                  tok(d),
                  pl.BlockSpec((1, 6, d), lambda bi, i: (bi, 0, 0)),
                  pl.BlockSpec((1, d), lambda bi, i: (0, 0))],
        out_specs=tok(d),
        out_shape=jax.ShapeDtypeStruct((b, s, d), F32),
        compiler_params=_params(("arbitrary", "arbitrary")),
        name="combine",
    )(rankc, y2, x1, mod, g_final.reshape(1, d))


def _rope_tables(s):
    half = MLA_ROPE_DIM // 2
    freqs = ROPE_THETA ** (-jnp.arange(half, dtype=F32) / half)
    t = jnp.arange(s)
    ang = lambda pos: pos.astype(F32)[:, None] * freqs[None, :]
    a1, ar, ac = ang(t), ang(t // GRID_W), ang(t % GRID_W)
    z = jnp.zeros((s, half), F32)
    cat = lambda *p: jnp.concatenate(p, axis=1)
    c1 = cat(jnp.cos(a1), z, jnp.cos(a1), z)
    s1 = cat(-jnp.sin(a1), z, jnp.sin(a1), z)
    c2 = cat(jnp.cos(ar), jnp.cos(ac), jnp.cos(ar), jnp.cos(ac))
    s2 = cat(-jnp.sin(ar), -jnp.sin(ac), jnp.sin(ar), jnp.sin(ac))
    return c1, s1, c2, s2


def _pair_apart(w, heads):
    lead = w.shape[:-1]
    w = w.reshape(*lead, heads, 4, GQA_HEAD_DIM // 4)
    return w[..., jnp.array([0, 2, 1, 3]), :].reshape(*lead, heads * GQA_HEAD_DIM)


def _spread_rope(w):
    half = MLA_ROPE_DIM // 2
    z = jnp.zeros(w.shape[:-1] + (half,), w.dtype)
    return jnp.concatenate([w[..., :half], z, w[..., half:], z], axis=-1)


def _regroup_kernel(w_ref, o_ref, *, segments):
    quarter = GQA_HEAD_DIM // 4
    out = 0

    def move(src, rows):
        nonlocal out
        o_ref[out:out + rows, :] = w_ref[src:src + rows, :].astype(BF16)
        out += rows

    def zeros(rows):
        nonlocal out
        o_ref[out:out + rows, :] = jnp.zeros((rows, o_ref.shape[1]), BF16)
        out += rows

    for src, rows, kind in segments:
        if kind == "copy":
            move(src, rows)
        elif kind == "zero":
            zeros(rows)
        elif kind == "pair_apart":
            for head in range(0, rows, GQA_HEAD_DIM):
                for block in (0, 2, 1, 3):
                    move(src + head + block * quarter, quarter)
        else:
            half = rows // 2
            for part in range(2):
                move(src + part * half, half)
                zeros(half)
    assert out == o_ref.shape[0]


def _regroup_w_in(w_in):
    d, n = w_in.shape
    names = ("cq", "ckv", "kr", "qg", "kg", "vg", "ga", "gb")
    widths = (MLA_Q_RANK, MLA_KV_RANK, MLA_ROPE_DIM, GQA_HEADS * GQA_HEAD_DIM,
              GQA_KV_HEADS * GQA_HEAD_DIM, GQA_KV_HEADS * GQA_HEAD_DIM, d, d)
    src, off = {}, 0
    for name, w in zip(names, widths):
        src[name] = (off, w)
        off += w
    assert off == n
    segments = [src["ga"] + ("copy",), src["gb"] + ("copy",), src["qg"] + ("pair_apart",),
                src["kg"] + ("pair_apart",), src["vg"] + ("copy",), src["cq"] + ("copy",),
                src["ckv"] + ("copy",), src["kr"] + ("spread",)]
    used = sum(2 * w if kind == "spread" else w for _, w, kind in segments)
    segments.append((0, _Z_COLS - used, "zero"))
    cols = 256
    return pl.pallas_call(
        functools.partial(_regroup_kernel, segments=tuple(segments)),
        grid=(d // cols,),
        in_specs=[pl.BlockSpec((n, cols), lambda i: (0, i))],
        out_specs=pl.BlockSpec((_Z_COLS, cols), lambda i: (0, i)),
        out_shape=jax.ShapeDtypeStruct((_Z_COLS, d), BF16),
        compiler_params=_params(("arbitrary",)),
        name="regroup",
    )(jnp.swapaxes(w_in, 0, 1))


def _regroup_w_uq(w_uq):
    r = w_uq.shape[0]
    w = w_uq.astype(BF16).reshape(r, MLA_HEADS, MLA_NOPE_DIM + MLA_ROPE_DIM)
    w = jnp.concatenate([w[..., :MLA_NOPE_DIM], _spread_rope(w[..., MLA_NOPE_DIM:])], axis=-1)
    return w.reshape(r, MLA_HEADS * MLA_QK_PAD)


def _regroup_w_ukv(w_ukv):
    r = w_ukv.shape[0]
    w = w_ukv.reshape(r, MLA_HEADS, MLA_NOPE_DIM + MLA_V_DIM)
    wk = w[:, :, :MLA_NOPE_DIM].reshape(r, MLA_HEADS * MLA_NOPE_DIM)
    wv = w[:, :, MLA_NOPE_DIM:].reshape(r, MLA_HEADS * MLA_V_DIM)
    return jnp.concatenate([wk, wv], axis=1).astype(BF16)


def _split_router(w_router):
    d, e = w_router.shape
    w = jnp.pad(w_router, ((0, 0), (0, LANES - e)))
    hi = w.astype(BF16)
    lo = (w - hi.astype(F32)).astype(BF16)
    return jnp.concatenate([hi, lo], axis=1)


def kernel(x, c, w_ada, b_ada, g_mix, w_in, g_cq, g_ckv, w_uq, w_ukv, g_qn, g_kn, w_br_a, w_br_b, w_out, g_ffn, w_router, w_e_gate, w_e_up, w_e_down, g_final):
    b, s, d = x.shape
    depth = w_ada.shape[0]
    cap = EC_CAPACITY * s // N_EXPERTS
    assert d == _Z_GATE_B and s % LANES == 0 and cap % LANES == 0
    rows = lambda n: min(n, s)
    tabs = _rope_tables(s)
    for l in range(depth):
        mod = _adaln(c, w_ada[l], b_ada[l]).reshape(b, 6, d)
        z = _inproj(x, mod, g_mix[l], _regroup_w_in(w_in[l]), rows(INPROJ_ROWS))
        qm, km, vm, qb, kb, vb = _prep(z, g_cq[l], g_ckv[l], _pair_apart(g_qn[l], 1),
                                       _pair_apart(g_kn[l], 1), _regroup_w_uq(w_uq[l]),
                                       _regroup_w_ukv(w_ukv[l]), tabs, rows(PREP_ROWS))
        o_a = _attention(qm, km, vm, groups=MLA_HEADS // MLA_HEADS_PER_STEP,
                         heads=MLA_HEADS_PER_STEP, dk=MLA_QK_PAD, dv=MLA_V_DIM, shared_kv=False,
                         tq=rows(ATTN_Q_ROWS), name="mla_attn")
        o_b = _attention(qb, kb, vb, groups=GQA_KV_HEADS, heads=GQA_GROUP, dk=GQA_HEAD_DIM,
                         dv=GQA_HEAD_DIM, shared_kv=True, tq=rows(ATTN_Q_ROWS), name="gqa_attn")
        wr = _split_router(w_router[l])
        x1, h2, logits = _merge(o_a, o_b, z, x, mod, g_ffn[l], w_br_a[l].astype(BF16),
                                w_br_b[l].astype(BF16), w_out[l].astype(BF16), wr,
                                rows(MERGE_ROWS))
        rank, wval, rankc = _router(logits, cap)
        xg, val = _gather(rank, wval, h2, cap)
        y = _experts(xg, val, w_e_gate[l], w_e_up[l], w_e_down[l])
        x = _combine(rankc, y, x1, mod, g_final, cap, rows(COMBINE_ROWS),
                     final=(l == depth - 1))
    return x
```

```python
import functools

import jax
import jax.numpy as jnp
import numpy as np
from jax import lax
from jax.experimental import pallas as pl
from jax.experimental.pallas import tpu as pltpu

F32 = jnp.float32
BF16 = jnp.bfloat16

GRID_W = 64
ROPE_THETA = 10000.0
RMS_EPS = 1e-6

MLA_HEADS = 8
MLA_Q_RANK = 512
MLA_KV_RANK = 256
MLA_NOPE_DIM = 128
MLA_ROPE_DIM = 64
MLA_V_DIM = 128
MLA_QK_PAD = 256
MLA_HEADS_PER_STEP = 4
ATTN_KEY_CHUNKS = 4

GQA_HEADS = 8
GQA_KV_HEADS = 2
GQA_HEAD_DIM = 128
GQA_GROUP = GQA_HEADS // GQA_KV_HEADS

N_EXPERTS = 16
EC_CAPACITY = 2

LANES = 128

ADALN_COLS = 1024

INPROJ_ROWS = 512
NORM_CHUNK = 128
PREP_ROWS = 1024
ATTN_Q_ROWS = 1024
GATHER_EXPERTS_PER_STEP = 8
EXPERT_BATCHES_PER_STEP = 2
MERGE_ROWS = 256
COMBINE_ROWS = 256
V7X_VMEM_LIMIT = 56 * 1024 * 1024

_Z_GATE_A = 0
_Z_GATE_B = 2048
_Z_QG = 4096
_Z_KG = 5120
_Z_VG = 5376
_Z_CQ = 5632
_Z_CKV = 6144
_Z_KR = 6400
_Z_COLS = 6656
_Z_TILE = _Z_COLS // 2


def _params(sem, vmem=V7X_VMEM_LIMIT):
    return pltpu.CompilerParams(dimension_semantics=sem, vmem_limit_bytes=vmem)


def _resident(shape, index_map):
    return pl.BlockSpec(shape, index_map, pipeline_mode=pl.Buffered(1))


def _rms(x):
    return x * lax.rsqrt(jnp.mean(x * x, axis=-1, keepdims=True) + RMS_EPS)


def _rope128(x, cos, sin):
    return x * cos + pltpu.roll(x, LANES // 2, 1) * sin


def _adaln_kernel(c_ref, w_ref, b_ref, o_ref):
    c = c_ref[...]
    sc = (c * jax.nn.sigmoid(c)).astype(BF16)
    o_ref[...] = jnp.dot(sc, w_ref[...].astype(BF16), preferred_element_type=F32) + b_ref[...]


def _adaln(c, w_ada, b_ada):
    b, d = c.shape
    n = w_ada.shape[1]
    tn = ADALN_COLS
    return pl.pallas_call(
        _adaln_kernel,
        grid=(n // tn,),
        in_specs=[pl.BlockSpec((b, d), lambda j: (0, 0)),
                  pl.BlockSpec((d, tn), lambda j: (0, j)),
                  pl.BlockSpec((1, tn), lambda j: (0, j))],
        out_specs=pl.BlockSpec((b, tn), lambda j: (0, j)),
        out_shape=jax.ShapeDtypeStruct((b, n), F32),
        compiler_params=_params(("arbitrary",)),
        name="adaln",
    )(c, w_ada, b_ada.reshape(1, n))


def _inproj_kernel(x_ref, mod_ref, g_ref, w_ref, z_ref, h_scr):
    @pl.when(pl.program_id(2) == 0)
    def _():
        gain = g_ref[...] * (1.0 + mod_ref[0, 1:2, :])
        shift = mod_ref[0, 0:1, :]

        def body(i, carry):
            r = pl.ds(pl.multiple_of(i * NORM_CHUNK, NORM_CHUNK), NORM_CHUNK)
            h_scr[r, :] = (_rms(x_ref[0, r, :]) * gain + shift).astype(BF16)
            return carry

        lax.fori_loop(0, h_scr.shape[0] // NORM_CHUNK, body, 0)

    z_ref[0] = lax.dot_general(h_scr[...], w_ref[...], (((1,), (1,)), ((), ())),
                               preferred_element_type=F32).astype(BF16)


def _inproj(x, mod, g_mix, w_z, tm):
    b, s, d = x.shape
    nj = _Z_COLS // _Z_TILE
    return pl.pallas_call(
        _inproj_kernel,
        grid=(b, s // tm, nj),
        in_specs=[pl.BlockSpec((1, tm, d), lambda bi, i, j: (bi, i, 0)),
                  pl.BlockSpec((1, 6, d), lambda bi, i, j: (bi, 0, 0)),
                  pl.BlockSpec((1, d), lambda bi, i, j: (0, 0)),
                  pl.BlockSpec((_Z_TILE, d), lambda bi, i, j: (j, 0))],
        out_specs=pl.BlockSpec((1, tm, _Z_TILE), lambda bi, i, j: (bi, i, j)),
        out_shape=jax.ShapeDtypeStruct((b, s, _Z_COLS), BF16),
        scratch_shapes=[pltpu.VMEM((tm, d), BF16)],
        compiler_params=_params(("arbitrary", "arbitrary", "arbitrary")),
        name="inproj",
    )(x, mod, g_mix.reshape(1, d), w_z)


def _prep_kernel(cq_ref, ckv_ref, kr_ref, qg_ref, kg_ref, vg_ref,
                 gcq_ref, gckv_ref, gqn_ref, gkn_ref, wq_ref, wkv_ref,
                 c1_ref, s1_ref, c2_ref, s2_ref,
                 qm_ref, km_ref, vm_ref, qb_ref, kb_ref, vb_ref):
    c1, s1 = c1_ref[...], s1_ref[...]
    c2, s2 = c2_ref[...], s2_ref[...]
    ts = cq_ref.shape[1]
    one_col = (lax.broadcasted_iota(jnp.int32, (ts, LANES), 1) == 0).astype(BF16)

    cqn = (_rms(cq_ref[0].astype(F32)) * gcq_ref[...]).astype(BF16)
    q = jnp.dot(cqn, wq_ref[...], preferred_element_type=F32)
    q = q * ((MLA_NOPE_DIM + MLA_ROPE_DIM) ** -0.5)
    for h in range(MLA_HEADS):
        lo = h * MLA_QK_PAD
        qm_ref[0, :, lo:lo + LANES] = q[:, lo:lo + LANES].astype(BF16)
        qm_ref[0, :, lo + LANES:lo + 2 * LANES] = _rope128(
            q[:, lo + LANES:lo + 2 * LANES], c1, s1).astype(BF16)

    ckvn = (_rms(ckv_ref[0].astype(F32)) * gckv_ref[...]).astype(BF16)
    kv = jnp.dot(ckvn, wkv_ref[...], preferred_element_type=F32)
    krr = _rope128(kr_ref[0].astype(F32), c1, s1).astype(BF16)
    nk = MLA_HEADS * MLA_NOPE_DIM
    for h in range(MLA_HEADS):
        lo = h * MLA_QK_PAD
        km_ref[0, :, lo:lo + LANES] = kv[:, h * LANES:(h + 1) * LANES].astype(BF16)
        km_ref[0, :, lo + LANES:lo + 2 * LANES] = krr
    for h in range(MLA_HEADS):
        lo = 2 * h * MLA_V_DIM
        vm_ref[0, :, lo:lo + MLA_V_DIM] = kv[:, nk + h * MLA_V_DIM:nk + (h + 1) * MLA_V_DIM].astype(BF16)
        vm_ref[0, :, lo + MLA_V_DIM:lo + 2 * MLA_V_DIM] = one_col

    gqn = gqn_ref[...] * (GQA_HEAD_DIM ** -0.5)
    for h in range(GQA_HEADS):
        lo = h * GQA_HEAD_DIM
        t = _rms(qg_ref[0, :, lo:lo + GQA_HEAD_DIM].astype(F32)) * gqn
        qb_ref[0, :, lo:lo + GQA_HEAD_DIM] = _rope128(t, c2, s2).astype(BF16)
    for h in range(GQA_KV_HEADS):
        lo = h * GQA_HEAD_DIM
        t = _rms(kg_ref[0, :, lo:lo + GQA_HEAD_DIM].astype(F32)) * gkn_ref[...]
        kb_ref[0, :, lo:lo + GQA_HEAD_DIM] = _rope128(t, c2, s2).astype(BF16)
        vb_ref[0, :, 2 * lo:2 * lo + GQA_HEAD_DIM] = vg_ref[0, :, lo:lo + GQA_HEAD_DIM]
        vb_ref[0, :, 2 * lo + GQA_HEAD_DIM:2 * lo + 2 * GQA_HEAD_DIM] = one_col


def _prep(z, g_cq, g_ckv, g_qn, g_kn, wq, wkv, tabs, ts):
    b, s, _ = z.shape

    def zspec(width, col):
        return pl.BlockSpec((1, ts, width), lambda bi, i: (bi, i, col // width))

    def full(a):
        return pl.BlockSpec(a.shape, lambda bi, i: (0,) * a.ndim)

    tab_spec = pl.BlockSpec((ts, LANES), lambda bi, i: (i, 0))
    row = lambda g: g.reshape(1, -1)
    gs = [row(g_cq), row(g_ckv), row(g_qn), row(g_kn)]
    qk = MLA_HEADS * MLA_QK_PAD

    def ospec(width):
        return pl.BlockSpec((1, ts, width), lambda bi, i: (bi, i, 0))

    return pl.pallas_call(
        _prep_kernel,
        grid=(b, s // ts),
        in_specs=[zspec(MLA_Q_RANK, _Z_CQ), zspec(MLA_KV_RANK, _Z_CKV), zspec(LANES, _Z_KR),
                  zspec(GQA_HEADS * GQA_HEAD_DIM, _Z_QG), zspec(GQA_KV_HEADS * GQA_HEAD_DIM, _Z_KG),
                  zspec(GQA_KV_HEADS * GQA_HEAD_DIM, _Z_VG)]
                 + [full(g) for g in gs] + [full(wq), full(wkv)] + [tab_spec] * 4,
        out_specs=[ospec(qk), ospec(qk), ospec(2 * MLA_HEADS * MLA_V_DIM),
                   ospec(GQA_HEADS * GQA_HEAD_DIM), ospec(GQA_KV_HEADS * GQA_HEAD_DIM),
                   ospec(2 * GQA_KV_HEADS * GQA_HEAD_DIM)],
        out_shape=[jax.ShapeDtypeStruct((b, s, qk), BF16),
                   jax.ShapeDtypeStruct((b, s, qk), BF16),
                   jax.ShapeDtypeStruct((b, s, 2 * MLA_HEADS * MLA_V_DIM), BF16),
                   jax.ShapeDtypeStruct((b, s, GQA_HEADS * GQA_HEAD_DIM), BF16),
                   jax.ShapeDtypeStruct((b, s, GQA_KV_HEADS * GQA_HEAD_DIM), BF16),
                   jax.ShapeDtypeStruct((b, s, 2 * GQA_KV_HEADS * GQA_HEAD_DIM), BF16)],
        compiler_params=_params(("arbitrary", "arbitrary")),
        name="prep",
    )(z, z, z, z, z, z, *gs, wq, wkv, *tabs)


def _attn_kernel(q_ref, k_ref, v_ref, o_ref, *, heads, dk, dv, shared_kv):
    tk = k_ref.shape[1] // ATTN_KEY_CHUNKS
    for g in range(heads):
        kv = 0 if shared_kv else g
        q = q_ref[0, :, g * dk:(g + 1) * dk]
        m = acc = None
        for c in range(ATTN_KEY_CHUNKS):
            k = k_ref[0, c * tk:(c + 1) * tk, kv * dk:(kv + 1) * dk]
            v = v_ref[0, c * tk:(c + 1) * tk, 2 * kv * dv:2 * (kv + 1) * dv]
            s = lax.dot_general(q, k, (((1,), (1,)), ((), ())), preferred_element_type=F32)
            mc = jnp.max(s, axis=-1, keepdims=True)
            m_new = mc if m is None else jnp.maximum(m, mc)
            pv = jnp.dot(jnp.exp(s - m_new).astype(BF16), v, preferred_element_type=F32)
            acc = pv if acc is None else acc * jnp.exp(m - m_new) + pv
            m = m_new
        o_ref[0, :, g * dv:(g + 1) * dv] = (acc[:, :dv] / acc[:, dv:dv + 1]).astype(BF16)


def _attention(q, k, v, *, groups, heads, dk, dv, shared_kv, tq, name):
    b, s, _ = q.shape
    kvh = 1 if shared_kv else heads
    return pl.pallas_call(
        functools.partial(_attn_kernel, heads=heads, dk=dk, dv=dv, shared_kv=shared_kv),
        grid=(b, groups, s // tq),
        in_specs=[pl.BlockSpec((1, tq, heads * dk), lambda bi, g, i: (bi, i, g)),
                  pl.BlockSpec((1, s, kvh * dk), lambda bi, g, i: (bi, 0, g)),
                  pl.BlockSpec((1, s, kvh * 2 * dv), lambda bi, g, i: (bi, 0, g))],
        out_specs=pl.BlockSpec((1, tq, heads * dv), lambda bi, g, i: (bi, i, g)),
        out_shape=jax.ShapeDtypeStruct((b, s, groups * heads * dv), BF16),
        compiler_params=_params(("arbitrary", "arbitrary", "arbitrary")),
        name=name,
    )(q, k, v)


def _merge_kernel(oa_ref, ob_ref, ga_ref, gb_ref, x_ref, mod_ref, gffn_ref,
                  wa_ref, wb_ref, wo_ref, wr_ref,
                  x1_ref, h2_ref, lg_ref):
    a = jnp.dot(oa_ref[0], wa_ref[...], preferred_element_type=F32)
    bb = jnp.dot(ob_ref[0], wb_ref[...], preferred_element_type=F32)
    merged = (jax.nn.sigmoid(ga_ref[0].astype(F32)) * a
              + jax.nn.sigmoid(gb_ref[0].astype(F32)) * bb).astype(BF16)
    y = jnp.dot(merged, wo_ref[...], preferred_element_type=F32)
    x1 = x_ref[0] + mod_ref[0, 2:3, :] * y
    x1_ref[0] = x1
    h2 = _rms(x1) * gffn_ref[...] * (1.0 + mod_ref[0, 4:5, :]) + mod_ref[0, 3:4, :]
    hi = h2.astype(BF16)
    lo = (h2 - hi.astype(F32)).astype(BF16)
    h2_ref[0] = hi
    lg = jnp.dot(jnp.concatenate([hi, lo], axis=0), wr_ref[...], preferred_element_type=F32)
    tm = hi.shape[0]
    lg = lg[:tm] + lg[tm:]
    lg_ref[0] = lg[:, :LANES] + lg[:, LANES:]


def _merge(o_a, o_b, z, x, mod, g_ffn, wa, wb, wo, wr, tm):
    b, s, d = x.shape
    na, nb = o_a.shape[2], o_b.shape[2]
    tok = lambda width, col=0: pl.BlockSpec((1, tm, width), lambda bi, i: (bi, i, col // width))
    res = lambda a: _resident(a.shape, lambda bi, i: (0,) * a.ndim)
    return pl.pallas_call(
        _merge_kernel,
        grid=(b, s // tm),
        in_specs=[tok(na), tok(nb), tok(d, _Z_GATE_A), tok(d, _Z_GATE_B), tok(d),
                  pl.BlockSpec((1, 6, d), lambda bi, i: (bi, 0, 0)),
                  pl.BlockSpec((1, d), lambda bi, i: (0, 0)),
                  res(wa), res(wb), res(wo), res(wr)],
        out_specs=[tok(d), tok(d), tok(LANES)],
        out_shape=[jax.ShapeDtypeStruct((b, s, d), F32),
                   jax.ShapeDtypeStruct((b, s, d), BF16),
                   jax.ShapeDtypeStruct((b, s, LANES), F32)],
        compiler_params=_params(("arbitrary", "arbitrary")),
        name="merge",
    )(o_a, o_b, z, z, x, mod, g_ffn.reshape(1, d), wa, wb, wo, wr)


def _lane_cumsum(x):
    n = x.shape[1]
    lane = lax.broadcasted_iota(jnp.int32, x.shape, 1)
    d = 1
    while d < n:
        x = x + jnp.where(lane >= d, pltpu.roll(x, d, 1), 0.0)
        d *= 2
    return x


def _router_kernel(lg_ref, rank_ref, wval_ref, rankc_ref, *, cap):
    e = N_EXPERTS
    nb, s = lg_ref.shape[:2]
    affs = []
    for bi in range(nb):
        lt = jnp.transpose(lg_ref[bi])[:e, :]
        ex = jnp.exp(lt - jnp.max(lt, axis=0, keepdims=True))
        affs.append(ex / jnp.sum(ex, axis=0, keepdims=True))
    aff = jnp.concatenate(affs, axis=0)
    rows = nb * e

    def step(_, carry):
        lo, hi = carry
        mid = lo + ((hi - lo) >> 1)
        cnt = jnp.sum((aff >= pltpu.bitcast(mid, F32)).astype(jnp.int32), axis=1, keepdims=True)
        ok = cnt >= cap
        return jnp.where(ok, mid, lo), jnp.where(ok, hi, mid)

    lo0 = jnp.zeros((rows, 1), jnp.int32)
    hi0 = jnp.full((rows, 1), 0x3F800001, jnp.int32)
    thr_bits, _ = lax.fori_loop(0, 31, step, (lo0, hi0))
    thr = pltpu.bitcast(thr_bits, F32)

    gt = aff > thr
    eq = aff == thr
    need = (cap - jnp.sum(gt.astype(jnp.int32), axis=1, keepdims=True)).astype(F32)
    ceq = _lane_cumsum(eq.astype(F32))
    sel = gt | (eq & (ceq <= need))
    rank = jnp.where(sel, _lane_cumsum(sel.astype(F32)) - 1.0, -1.0)
    wval = jnp.where(sel, aff, 0.0)
    pad = jnp.full((LANES - e, s), -1.0, F32)
    for bi in range(nb):
        rank_b = rank[bi * e:(bi + 1) * e]
        rank_ref[bi] = rank_b
        wval_ref[bi] = wval[bi * e:(bi + 1) * e]
        rankc_ref[bi] = jnp.transpose(jnp.concatenate([rank_b, pad], axis=0))


def _router(logits, cap):
    b, s, _ = logits.shape
    e = N_EXPERTS
    whole = lambda *shape: pl.BlockSpec(shape, lambda i: (0,) * len(shape))
    return pl.pallas_call(
        functools.partial(_router_kernel, cap=cap),
        grid=(1,),
        in_specs=[whole(b, s, LANES)],
        out_specs=[whole(b, e, s), whole(b, e, s), whole(b, s, LANES)],
        out_shape=[jax.ShapeDtypeStruct((b, e, s), F32),
                   jax.ShapeDtypeStruct((b, e, s), F32),
                   jax.ShapeDtypeStruct((b, s, LANES), F32)],
        compiler_params=_params(("arbitrary",)),
        name="router",
    )(logits)


def _gather_kernel(rank_ref, wval_ref, h2_ref, xg_ref, val_ref, *, cap):
    s = rank_ref.shape[3]
    r_iota = lax.broadcasted_iota(jnp.int32, (cap, s), 0).astype(F32)
    for k in range(GATHER_EXPERTS_PER_STEP):
        hot = rank_ref[0, k] == r_iota
        xg_ref[0, k] = jnp.dot(hot.astype(BF16), h2_ref[0],
                               preferred_element_type=F32).astype(BF16)
        val_ref[0, k] = jnp.sum(jnp.where(hot, wval_ref[0, k], 0.0), axis=1, keepdims=True)


def _gather(rank, wval, h2, cap):
    b, e, s = rank.shape
    d = h2.shape[2]
    ge = GATHER_EXPERTS_PER_STEP
    assert e % ge == 0
    r4 = rank.reshape(b, e, 1, s)
    w4 = wval.reshape(b, e, 1, s)
    row = pl.BlockSpec((1, ge, 1, s), lambda bi, ei: (bi, ei, 0, 0))
    return pl.pallas_call(
        functools.partial(_gather_kernel, cap=cap),
        grid=(b, e // ge),
        in_specs=[row, row, pl.BlockSpec((1, s, d), lambda bi, ei: (bi, 0, 0))],
        out_specs=[pl.BlockSpec((1, ge, cap, d), lambda bi, ei: (bi, ei, 0, 0)),
                   pl.BlockSpec((1, ge, cap, 1), lambda bi, ei: (bi, ei, 0, 0))],
        out_shape=[jax.ShapeDtypeStruct((b, e, cap, d), BF16),
                   jax.ShapeDtypeStruct((b, e, cap, 1), F32)],
        compiler_params=_params(("arbitrary", "arbitrary")),
        name="gather",
    )(r4, w4, h2)


def _expert_kernel(xg_ref, val_ref, wg_ref, wu_ref, wd_ref, y_ref, wg_s, wu_s, wd_s, *, n_exp):
    p = pl.program_id(0)
    bi = pl.program_id(1)
    stage = p % 2
    ready = 1 - stage

    def stage_chunk():
        for w_ref, w_s in ((wg_ref, wg_s), (wu_ref, wu_s), (wd_ref, wd_s)):
            rows = w_ref.shape[1]
            w_s[stage, pl.ds(pl.multiple_of(bi * rows, rows), rows), :] = w_ref[0].astype(BF16)

    def run_expert():
        nb, cap = xg_ref.shape[0], xg_ref.shape[2]
        xg = jnp.concatenate([xg_ref[k, 0] for k in range(nb)], axis=0)
        a = jnp.dot(xg, wg_s[ready], preferred_element_type=F32)
        u = jnp.dot(xg, wu_s[ready], preferred_element_type=F32)
        hm = (a * jax.nn.sigmoid(a) * u).astype(BF16)
        y = jnp.dot(hm, wd_s[ready], preferred_element_type=F32)
        for k in range(nb):
            y_ref[k, 0] = (y[k * cap:(k + 1) * cap] * val_ref[k, 0]).astype(BF16)

    pl.when(p == 0)(stage_chunk)

    @pl.when((p > 0) & (p < n_exp))
    def _():
        stage_chunk()
        run_expert()

    pl.when(p == n_exp)(run_expert)


def _experts(xg, val, wg, wu, wd):
    b, e, cap, d = xg.shape
    f = wg.shape[2]
    nbs = EXPERT_BATCHES_PER_STEP
    steps = b // nbs
    assert b % nbs == 0 and d % steps == 0 and f % steps == 0
    last = e - 1
    tok = lambda p, bi: (jnp.where(p == 0, 0, bi), jnp.maximum(p - 1, 0), 0, 0)
    chunk = lambda p, bi: (jnp.minimum(p, last), jnp.where(p <= last, bi, steps - 1), 0)
    return pl.pallas_call(
        functools.partial(_expert_kernel, n_exp=e),
        grid=(e + 1, steps),
        in_specs=[pl.BlockSpec((nbs, 1, cap, d), tok),
                  pl.BlockSpec((nbs, 1, cap, 1), tok),
                  pl.BlockSpec((1, d // steps, f), chunk),
                  pl.BlockSpec((1, d // steps, f), chunk),
                  pl.BlockSpec((1, f // steps, d), chunk)],
        out_specs=pl.BlockSpec((nbs, 1, cap, d), tok),
        out_shape=jax.ShapeDtypeStruct((b, e, cap, d), BF16),
        scratch_shapes=[pltpu.VMEM((2, d, f), BF16),
                        pltpu.VMEM((2, d, f), BF16),
                        pltpu.VMEM((2, f, d), BF16)],
        compiler_params=_params(("arbitrary", "arbitrary")),
        name="experts",
    )(xg, val, wg, wu, wd)


def _combine_kernel(rankc_ref, y_ref, x1_ref, mod_ref, gfin_ref, o_ref, *, cap, final):
    tm = rankc_ref.shape[1]
    rc = rankc_ref[0]
    col = lax.broadcasted_iota(jnp.int32, (tm, cap), 1).astype(F32)
    hot = jnp.concatenate(
        [(rc[:, e:e + 1] == col).astype(BF16) for e in range(N_EXPERTS)], axis=1)
    moe = jnp.dot(hot, y_ref[0], preferred_element_type=F32)
    x2 = x1_ref[0] + mod_ref[0, 5:6, :] * moe
    if final:
        x2 = _rms(x2) * gfin_ref[...]
    o_ref[0] = x2


def _combine(rankc, y, x1, mod, g_final, cap, tm, final):
    b, s, d = x1.shape
    y2 = y.reshape(b, N_EXPERTS * cap, d)
    tok = lambda width: pl.BlockSpec((1, tm, width), lambda bi, i: (bi, i, 0))
    return pl.pallas_call(
        functools.partial(_combine_kernel, cap=cap, final=final),
        grid=(b, s // tm),
        in_specs=[tok(LANES),
                  pl.BlockSpec((1, N_EXPERTS * cap, d), lambda bi, i: (bi, 0, 0)),
                  tok(d),
                  pl.BlockSpec((1, 6, d), lambda bi, i: (bi, 0, 0)),
                  pl.BlockSpec((1, d), lambda bi, i: (0, 0))],
        out_specs=tok(d),
        out_shape=jax.ShapeDtypeStruct((b, s, d), F32),
        compiler_params=_params(("arbitrary", "arbitrary")),
        name="combine",
    )(rankc, y2, x1, mod, g_final.reshape(1, d))


def _rope_tables(s):
    half = MLA_ROPE_DIM // 2
    f32 = np.float32
    freqs = f32(ROPE_THETA) ** (-np.arange(half, dtype=f32) / f32(half))
    t = np.arange(s)
    ang = lambda pos: pos.astype(f32)[:, None] * freqs[None, :]
    a1, ar, ac = ang(t), ang(t // GRID_W), ang(t % GRID_W)
    z = np.zeros((s, half), f32)
    cat = lambda *p: jnp.asarray(np.concatenate(p, axis=1), F32)
    c1 = cat(np.cos(a1), z, np.cos(a1), z)
    s1 = cat(-np.sin(a1), z, np.sin(a1), z)
    c2 = cat(np.cos(ar), np.cos(ac), np.cos(ar), np.cos(ac))
    s2 = cat(-np.sin(ar), -np.sin(ac), np.sin(ar), np.sin(ac))
    return c1, s1, c2, s2


def _pair_apart(w, heads):
    lead = w.shape[:-1]
    w = w.reshape(*lead, heads, 4, GQA_HEAD_DIM // 4)
    return w[..., jnp.array([0, 2, 1, 3]), :].reshape(*lead, heads * GQA_HEAD_DIM)


def _spread_rope(w):
    half = MLA_ROPE_DIM // 2
    z = jnp.zeros(w.shape[:-1] + (half,), w.dtype)
    return jnp.concatenate([w[..., :half], z, w[..., half:], z], axis=-1)


def _regroup_kernel(w_ref, o_ref, *, segments):
    quarter = GQA_HEAD_DIM // 4
    out = 0

    def move(src, rows):
        nonlocal out
        o_ref[out:out + rows, :] = w_ref[src:src + rows, :].astype(BF16)
        out += rows

    def zeros(rows):
        nonlocal out
        o_ref[out:out + rows, :] = jnp.zeros((rows, o_ref.shape[1]), BF16)
        out += rows

    for src, rows, kind in segments:
        if kind == "copy":
            move(src, rows)
        elif kind == "zero":
            zeros(rows)
        elif kind == "pair_apart":
            for head in range(0, rows, GQA_HEAD_DIM):
                for block in (0, 2, 1, 3):
                    move(src + head + block * quarter, quarter)
        else:
            half = rows // 2
            for part in range(2):
                move(src + part * half, half)
                zeros(half)
    assert out == o_ref.shape[0]


def _regroup_w_in(w_in):
    d, n = w_in.shape
    names = ("cq", "ckv", "kr", "qg", "kg", "vg", "ga", "gb")
    widths = (MLA_Q_RANK, MLA_KV_RANK, MLA_ROPE_DIM, GQA_HEADS * GQA_HEAD_DIM,
              GQA_KV_HEADS * GQA_HEAD_DIM, GQA_KV_HEADS * GQA_HEAD_DIM, d, d)
    src, off = {}, 0
    for name, w in zip(names, widths):
        src[name] = (off, w)
        off += w
    assert off == n
    segments = [src["ga"] + ("copy",), src["gb"] + ("copy",), src["qg"] + ("pair_apart",),
                src["kg"] + ("pair_apart",), src["vg"] + ("copy",), src["cq"] + ("copy",),
                src["ckv"] + ("copy",), src["kr"] + ("spread",)]
    used = sum(2 * w if kind == "spread" else w for _, w, kind in segments)
    segments.append((0, _Z_COLS - used, "zero"))
    cols = 256
    return pl.pallas_call(
        functools.partial(_regroup_kernel, segments=tuple(segments)),
        grid=(d // cols,),
        in_specs=[pl.BlockSpec((n, cols), lambda i: (0, i))],
        out_specs=pl.BlockSpec((_Z_COLS, cols), lambda i: (0, i)),
        out_shape=jax.ShapeDtypeStruct((_Z_COLS, d), BF16),
        compiler_params=_params(("arbitrary",)),
        name="regroup",
    )(jnp.swapaxes(w_in, 0, 1))


def _regroup_w_uq(w_uq):
    r = w_uq.shape[0]
    w = w_uq.astype(BF16).reshape(r, MLA_HEADS, MLA_NOPE_DIM + MLA_ROPE_DIM)
    w = jnp.concatenate([w[..., :MLA_NOPE_DIM], _spread_rope(w[..., MLA_NOPE_DIM:])], axis=-1)
    return w.reshape(r, MLA_HEADS * MLA_QK_PAD)


def _regroup_w_ukv(w_ukv):
    r = w_ukv.shape[0]
    w = w_ukv.reshape(r, MLA_HEADS, MLA_NOPE_DIM + MLA_V_DIM)
    wk = w[:, :, :MLA_NOPE_DIM].reshape(r, MLA_HEADS * MLA_NOPE_DIM)
    wv = w[:, :, MLA_NOPE_DIM:].reshape(r, MLA_HEADS * MLA_V_DIM)
    return jnp.concatenate([wk, wv], axis=1).astype(BF16)


def _split_router(w_router):
    d, e = w_router.shape
    w = jnp.pad(w_router, ((0, 0), (0, LANES - e)))
    hi = w.astype(BF16)
    lo = (w - hi.astype(F32)).astype(BF16)
    return jnp.concatenate([hi, lo], axis=1)


def kernel(x, c, w_ada, b_ada, g_mix, w_in, g_cq, g_ckv, w_uq, w_ukv, g_qn, g_kn, w_br_a, w_br_b, w_out, g_ffn, w_router, w_e_gate, w_e_up, w_e_down, g_final):
    b, s, d = x.shape
    depth = w_ada.shape[0]
    cap = EC_CAPACITY * s // N_EXPERTS
    assert d == _Z_GATE_B and s % LANES == 0 and cap % LANES == 0
    rows = lambda n: min(n, s)
    tabs = _rope_tables(s)
    for l in range(depth):
        mod = _adaln(c, w_ada[l], b_ada[l]).reshape(b, 6, d)
        z = _inproj(x, mod, g_mix[l], _regroup_w_in(w_in[l]), rows(INPROJ_ROWS))
        qm, km, vm, qb, kb, vb = _prep(z, g_cq[l], g_ckv[l], _pair_apart(g_qn[l], 1),
                                       _pair_apart(g_kn[l], 1), _regroup_w_uq(w_uq[l]),
                                       _regroup_w_ukv(w_ukv[l]), tabs, rows(PREP_ROWS))
        o_a = _attention(qm, km, vm, groups=MLA_HEADS // MLA_HEADS_PER_STEP,
                         heads=MLA_HEADS_PER_STEP, dk=MLA_QK_PAD, dv=MLA_V_DIM, shared_kv=False,
                         tq=rows(ATTN_Q_ROWS), name="mla_attn")
        o_b = _attention(qb, kb, vb, groups=GQA_KV_HEADS, heads=GQA_GROUP, dk=GQA_HEAD_DIM,
                         dv=GQA_HEAD_DIM, shared_kv=True, tq=rows(ATTN_Q_ROWS), name="gqa_attn")
        wr = _split_router(w_router[l])
        x1, h2, logits = _merge(o_a, o_b, z, x, mod, g_ffn[l], w_br_a[l].astype(BF16),
                                w_br_b[l].astype(BF16), w_out[l].astype(BF16), wr,
                                rows(MERGE_ROWS))
        rank, wval, rankc = _router(logits, cap)
        xg, val = _gather(rank, wval, h2, cap)
        y = _experts(xg, val, w_e_gate[l], w_e_up[l], w_e_down[l])
        x = _combine(rankc, y, x1, mod, g_final, cap, rows(COMBINE_ROWS),
                     final=(l == depth - 1))
    return x
```

```python
import functools

import jax
import jax.numpy as jnp
import numpy as np
from jax import lax
from jax.experimental import pallas as pl
from jax.experimental.pallas import tpu as pltpu

F32 = jnp.float32
BF16 = jnp.bfloat16

GRID_W = 64
ROPE_THETA = 10000.0
RMS_EPS = 1e-6

MLA_HEADS = 8
MLA_Q_RANK = 512
MLA_KV_RANK = 256
MLA_NOPE_DIM = 128
MLA_ROPE_DIM = 64
MLA_V_DIM = 128
MLA_QK_PAD = 256
MLA_HEADS_PER_STEP = 4
ATTN_KEY_CHUNKS = 4

GQA_HEADS = 8
GQA_KV_HEADS = 2
GQA_HEAD_DIM = 128
GQA_GROUP = GQA_HEADS // GQA_KV_HEADS

N_EXPERTS = 16
EC_CAPACITY = 2

LANES = 128

ADALN_COLS = 1024

INPROJ_ROWS = 512
NORM_CHUNK = 128
PREP_ROWS = 1024
ATTN_Q_ROWS = 1024
GATHER_EXPERTS_PER_STEP = 8
EXPERT_BATCHES_PER_STEP = 2
MERGE_ROWS = 256
COMBINE_ROWS = 256
V7X_VMEM_LIMIT = 56 * 1024 * 1024

_Z_GATE_A = 0
_Z_GATE_B = 2048
_Z_QG = 4096
_Z_KG = 5120
_Z_VG = 5376
_Z_CQ = 5632
_Z_CKV = 6144
_Z_KR = 6400
_Z_COLS = 6656
_Z_TILE = _Z_COLS // 2


def _params(sem, vmem=V7X_VMEM_LIMIT):
    return pltpu.CompilerParams(dimension_semantics=sem, vmem_limit_bytes=vmem)


def _resident(shape, index_map):
    return pl.BlockSpec(shape, index_map, pipeline_mode=pl.Buffered(1))


def _rms(x):
    return x * lax.rsqrt(jnp.mean(x * x, axis=-1, keepdims=True) + RMS_EPS)


def _rope128(x, cos, sin):
    return x * cos + pltpu.roll(x, LANES // 2, 1) * sin


def _adaln_kernel(c_ref, w_ref, b_ref, o_ref):
    c = c_ref[...]
    sc = (c * jax.nn.sigmoid(c)).astype(BF16)
    o_ref[...] = jnp.dot(sc, w_ref[...].astype(BF16), preferred_element_type=F32) + b_ref[...]


def _adaln(c, w_ada, b_ada):
    b, d = c.shape
    n = w_ada.shape[1]
    tn = ADALN_COLS
    return pl.pallas_call(
        _adaln_kernel,
        grid=(n // tn,),
        in_specs=[pl.BlockSpec((b, d), lambda j: (0, 0)),
                  pl.BlockSpec((d, tn), lambda j: (0, j)),
                  pl.BlockSpec((1, tn), lambda j: (0, j))],
        out_specs=pl.BlockSpec((b, tn), lambda j: (0, j)),
        out_shape=jax.ShapeDtypeStruct((b, n), F32),
        compiler_params=_params(("arbitrary",)),
        name="adaln",
    )(c, w_ada, b_ada.reshape(1, n))


def _inproj_kernel(x_ref, mod_ref, g_ref, w_ref, z_ref, h_scr):
    @pl.when(pl.program_id(2) == 0)
    def _():
        gain = g_ref[...] * (1.0 + mod_ref[0, 1:2, :])
        shift = mod_ref[0, 0:1, :]

        def body(i, carry):
            r = pl.ds(pl.multiple_of(i * NORM_CHUNK, NORM_CHUNK), NORM_CHUNK)
            h_scr[r, :] = (_rms(x_ref[0, r, :]) * gain + shift).astype(BF16)
            return carry

        lax.fori_loop(0, h_scr.shape[0] // NORM_CHUNK, body, 0)

    z_ref[0] = lax.dot_general(h_scr[...], w_ref[...], (((1,), (1,)), ((), ())),
                               preferred_element_type=F32).astype(BF16)


def _inproj(x, mod, g_mix, w_z, tm):
    b, s, d = x.shape
    nj = _Z_COLS // _Z_TILE
    return pl.pallas_call(
        _inproj_kernel,
        grid=(b, s // tm, nj),
        in_specs=[pl.BlockSpec((1, tm, d), lambda bi, i, j: (bi, i, 0)),
                  pl.BlockSpec((1, 6, d), lambda bi, i, j: (bi, 0, 0)),
                  pl.BlockSpec((1, d), lambda bi, i, j: (0, 0)),
                  pl.BlockSpec((_Z_TILE, d), lambda bi, i, j: (j, 0))],
        out_specs=pl.BlockSpec((1, tm, _Z_TILE), lambda bi, i, j: (bi, i, j)),
        out_shape=jax.ShapeDtypeStruct((b, s, _Z_COLS), BF16),
        scratch_shapes=[pltpu.VMEM((tm, d), BF16)],
        compiler_params=_params(("arbitrary", "arbitrary", "arbitrary")),
        name="inproj",
    )(x, mod, g_mix.reshape(1, d), w_z)


def _prep_kernel(cq_ref, ckv_ref, kr_ref, qg_ref, kg_ref,
                 gcq_ref, gckv_ref, gqn_ref, gkn_ref, wq_ref, wkv_ref,
                 c1_ref, s1_ref, c2_ref, s2_ref,
                 qm_ref, kn_ref, krr_ref, vm_ref, qb_ref, kb_ref):
    c1, s1 = c1_ref[...], s1_ref[...]
    c2, s2 = c2_ref[...], s2_ref[...]

    cqn = (_rms(cq_ref[0].astype(F32)) * gcq_ref[...]).astype(BF16)
    q = jnp.dot(cqn, wq_ref[...], preferred_element_type=F32)
    q = q * ((MLA_NOPE_DIM + MLA_ROPE_DIM) ** -0.5)
    for h in range(MLA_HEADS):
        lo = h * MLA_QK_PAD
        qm_ref[0, :, lo:lo + LANES] = q[:, lo:lo + LANES].astype(BF16)
        qm_ref[0, :, lo + LANES:lo + 2 * LANES] = _rope128(
            q[:, lo + LANES:lo + 2 * LANES], c1, s1).astype(BF16)

    ckvn = (_rms(ckv_ref[0].astype(F32)) * gckv_ref[...]).astype(BF16)
    kv = jnp.dot(ckvn, wkv_ref[...], preferred_element_type=F32)
    nk = MLA_HEADS * MLA_NOPE_DIM
    kn_ref[0] = kv[:, :nk].astype(BF16)
    vm_ref[0] = kv[:, nk:].astype(BF16)
    krr_ref[0] = _rope128(kr_ref[0].astype(F32), c1, s1).astype(BF16)

    gqn = gqn_ref[...] * (GQA_HEAD_DIM ** -0.5)
    for h in range(GQA_HEADS):
        lo = h * GQA_HEAD_DIM
        t = _rms(qg_ref[0, :, lo:lo + GQA_HEAD_DIM].astype(F32)) * gqn
        qb_ref[0, :, lo:lo + GQA_HEAD_DIM] = _rope128(t, c2, s2).astype(BF16)
    for h in range(GQA_KV_HEADS):
        lo = h * GQA_HEAD_DIM
        t = _rms(kg_ref[0, :, lo:lo + GQA_HEAD_DIM].astype(F32)) * gkn_ref[...]
        kb_ref[0, :, lo:lo + GQA_HEAD_DIM] = _rope128(t, c2, s2).astype(BF16)


def _prep(z, g_cq, g_ckv, g_qn, g_kn, wq, wkv, tabs, ts):
    b, s, _ = z.shape

    def zspec(width, col):
        return pl.BlockSpec((1, ts, width), lambda bi, i: (bi, i, col // width))

    def full(a):
        return pl.BlockSpec(a.shape, lambda bi, i: (0,) * a.ndim)

    tab_spec = pl.BlockSpec((ts, LANES), lambda bi, i: (i, 0))
    row = lambda g: g.reshape(1, -1)
    gs = [row(g_cq), row(g_ckv), row(g_qn), row(g_kn)]
    qk = MLA_HEADS * MLA_QK_PAD

    def ospec(width):
        return pl.BlockSpec((1, ts, width), lambda bi, i: (bi, i, 0))

    return pl.pallas_call(
        _prep_kernel,
        grid=(b, s // ts),
        in_specs=[zspec(MLA_Q_RANK, _Z_CQ), zspec(MLA_KV_RANK, _Z_CKV), zspec(LANES, _Z_KR),
                  zspec(GQA_HEADS * GQA_HEAD_DIM, _Z_QG), zspec(GQA_KV_HEADS * GQA_HEAD_DIM, _Z_KG)]
                 + [full(g) for g in gs] + [full(wq), full(wkv)] + [tab_spec] * 4,
        out_specs=[ospec(qk), ospec(MLA_HEADS * MLA_NOPE_DIM), ospec(LANES),
                   ospec(MLA_HEADS * MLA_V_DIM),
                   ospec(GQA_HEADS * GQA_HEAD_DIM), ospec(GQA_KV_HEADS * GQA_HEAD_DIM)],
        out_shape=[jax.ShapeDtypeStruct((b, s, qk), BF16),
                   jax.ShapeDtypeStruct((b, s, MLA_HEADS * MLA_NOPE_DIM), BF16),
                   jax.ShapeDtypeStruct((b, s, LANES), BF16),
                   jax.ShapeDtypeStruct((b, s, MLA_HEADS * MLA_V_DIM), BF16),
                   jax.ShapeDtypeStruct((b, s, GQA_HEADS * GQA_HEAD_DIM), BF16),
                   jax.ShapeDtypeStruct((b, s, GQA_KV_HEADS * GQA_HEAD_DIM), BF16)],
        compiler_params=_params(("arbitrary", "arbitrary")),
        name="prep",
    )(z, z, z, z, z, *gs, wq, wkv, *tabs)


def _attn_kernel(q_ref, k_ref, *rest, heads, dk, dv, shared_kv, shared_key_tile):
    kx_ref, v_ref, o_ref = rest if shared_key_tile else (None,) + rest
    tk = k_ref.shape[1] // ATTN_KEY_CHUNKS
    dkm = dk - LANES if shared_key_tile else dk
    one_col = (lax.broadcasted_iota(jnp.int32, (tk, LANES), 1) == 0).astype(BF16)
    for g in range(heads):
        kv = 0 if shared_kv else g
        q = q_ref[0, :, g * dk:(g + 1) * dk]
        m = acc = None
        for c in range(ATTN_KEY_CHUNKS):
            rows = slice(c * tk, (c + 1) * tk)
            k = k_ref[0, rows, kv * dkm:(kv + 1) * dkm]
            if shared_key_tile:
                k = jnp.concatenate([k, kx_ref[0, rows, :]], axis=1)
            v = jnp.concatenate([v_ref[0, rows, kv * dv:(kv + 1) * dv], one_col], axis=1)
            s = lax.dot_general(q, k, (((1,), (1,)), ((), ())), preferred_element_type=F32)
            mc = jnp.max(s, axis=-1, keepdims=True)
            m_new = mc if m is None else jnp.maximum(m, mc)
            pv = jnp.dot(jnp.exp(s - m_new).astype(BF16), v, preferred_element_type=F32)
            acc = pv if acc is None else acc * jnp.exp(m - m_new) + pv
            m = m_new
        o_ref[0, :, g * dv:(g + 1) * dv] = (acc[:, :dv] / acc[:, dv:dv + 1]).astype(BF16)


def _attention(q, k, kx, v, *, groups, heads, dk, dv, shared_kv, v_block0, tq, name):
    b, s, _ = q.shape
    kvh = 1 if shared_kv else heads
    dkm = dk - LANES if kx is not None else dk
    in_specs = [pl.BlockSpec((1, tq, heads * dk), lambda bi, g, i: (bi, i, g)),
                pl.BlockSpec((1, s, kvh * dkm), lambda bi, g, i: (bi, 0, g))]
    args = [q, k]
    if kx is not None:
        in_specs.append(pl.BlockSpec((1, s, LANES), lambda bi, g, i: (bi, 0, 0)))
        args.append(kx)
    in_specs.append(pl.BlockSpec((1, s, kvh * dv), lambda bi, g, i: (bi, 0, v_block0 + g)))
    args.append(v)
    return pl.pallas_call(
        functools.partial(_attn_kernel, heads=heads, dk=dk, dv=dv, shared_kv=shared_kv,
                          shared_key_tile=kx is not None),
        grid=(b, groups, s // tq),
        in_specs=in_specs,
        out_specs=pl.BlockSpec((1, tq, heads * dv), lambda bi, g, i: (bi, i, g)),
        out_shape=jax.ShapeDtypeStruct((b, s, groups * heads * dv), BF16),
        compiler_params=_params(("arbitrary", "arbitrary", "arbitrary")),
        name=name,
    )(*args)


def _merge_kernel(oa_ref, ob_ref, ga_ref, gb_ref, x_ref, mod_ref, gffn_ref,
                  wa_ref, wb_ref, wo_ref, wr_ref,
                  x1_ref, h2_ref, lg_ref):
    a = jnp.dot(oa_ref[0], wa_ref[...], preferred_element_type=F32)
    bb = jnp.dot(ob_ref[0], wb_ref[...], preferred_element_type=F32)
    merged = (jax.nn.sigmoid(ga_ref[0].astype(F32)) * a
              + jax.nn.sigmoid(gb_ref[0].astype(F32)) * bb).astype(BF16)
    y = jnp.dot(merged, wo_ref[...], preferred_element_type=F32)
    x1 = x_ref[0] + mod_ref[0, 2:3, :] * y
    x1_ref[0] = x1
    h2 = _rms(x1) * gffn_ref[...] * (1.0 + mod_ref[0, 4:5, :]) + mod_ref[0, 3:4, :]
    hi = h2.astype(BF16)
    lo = (h2 - hi.astype(F32)).astype(BF16)
    h2_ref[0] = hi
    lg = jnp.dot(jnp.concatenate([hi, lo], axis=0), wr_ref[...], preferred_element_type=F32)
    tm = hi.shape[0]
    lg = lg[:tm] + lg[tm:]
    lg_ref[0] = lg[:, :LANES] + lg[:, LANES:]


def _merge(o_a, o_b, z, x, mod, g_ffn, wa, wb, wo, wr, tm):
    b, s, d = x.shape
    na, nb = o_a.shape[2], o_b.shape[2]
    tok = lambda width, col=0: pl.BlockSpec((1, tm, width), lambda bi, i: (bi, i, col // width))
    res = lambda a: _resident(a.shape, lambda bi, i: (0,) * a.ndim)
    return pl.pallas_call(
        _merge_kernel,
        grid=(b, s // tm),
        in_specs=[tok(na), tok(nb), tok(d, _Z_GATE_A), tok(d, _Z_GATE_B), tok(d),
                  pl.BlockSpec((1, 6, d), lambda bi, i: (bi, 0, 0)),
                  pl.BlockSpec((1, d), lambda bi, i: (0, 0)),
                  res(wa), res(wb), res(wo), res(wr)],
        out_specs=[tok(d), tok(d), tok(LANES)],
        out_shape=[jax.ShapeDtypeStruct((b, s, d), F32),
                   jax.ShapeDtypeStruct((b, s, d), BF16),
                   jax.ShapeDtypeStruct((b, s, LANES), F32)],
        compiler_params=_params(("arbitrary", "arbitrary")),
        name="merge",
    )(o_a, o_b, z, z, x, mod, g_ffn.reshape(1, d), wa, wb, wo, wr)


def _lane_cumsum(x):
    n = x.shape[1]
    lane = lax.broadcasted_iota(jnp.int32, x.shape, 1)
    d = 1
    while d < n:
        x = x + jnp.where(lane >= d, pltpu.roll(x, d, 1), 0.0)
        d *= 2
    return x


def _router_kernel(lg_ref, rank_ref, wval_ref, rankc_ref, *, cap):
    e = N_EXPERTS
    nb, s = lg_ref.shape[:2]
    affs = []
    for bi in range(nb):
        lt = jnp.transpose(lg_ref[bi])[:e, :]
        ex = jnp.exp(lt - jnp.max(lt, axis=0, keepdims=True))
        affs.append(ex / jnp.sum(ex, axis=0, keepdims=True))
    aff = jnp.concatenate(affs, axis=0)
    rows = nb * e

    def step(_, carry):
        lo, hi = carry
        mid = lo + ((hi - lo) >> 1)
        cnt = jnp.sum((aff >= pltpu.bitcast(mid, F32)).astype(jnp.int32), axis=1, keepdims=True)
        ok = cnt >= cap
        return jnp.where(ok, mid, lo), jnp.where(ok, hi, mid)

    lo0 = jnp.zeros((rows, 1), jnp.int32)
    hi0 = jnp.full((rows, 1), 0x3F800001, jnp.int32)
    thr_bits, _ = lax.fori_loop(0, 31, step, (lo0, hi0))
    thr = pltpu.bitcast(thr_bits, F32)

    gt = aff > thr
    eq = aff == thr
    need = (cap - jnp.sum(gt.astype(jnp.int32), axis=1, keepdims=True)).astype(F32)
    ceq = _lane_cumsum(eq.astype(F32))
    sel = gt | (eq & (ceq <= need))
    rank = jnp.where(sel, _lane_cumsum(sel.astype(F32)) - 1.0, -1.0)
    wval = jnp.where(sel, aff, 0.0)
    pad = jnp.full((LANES - e, s), -1.0, F32)
    for bi in range(nb):
        rank_b = rank[bi * e:(bi + 1) * e]
        rank_ref[bi] = rank_b
        wval_ref[bi] = wval[bi * e:(bi + 1) * e]
        rankc_ref[bi] = jnp.transpose(jnp.concatenate([rank_b, pad], axis=0))


def _router(logits, cap):
    b, s, _ = logits.shape
    e = N_EXPERTS
    whole = lambda *shape: pl.BlockSpec(shape, lambda i: (0,) * len(shape))
    return pl.pallas_call(
        functools.partial(_router_kernel, cap=cap),
        grid=(1,),
        in_specs=[whole(b, s, LANES)],
        out_specs=[whole(b, e, s), whole(b, e, s), whole(b, s, LANES)],
        out_shape=[jax.ShapeDtypeStruct((b, e, s), F32),
                   jax.ShapeDtypeStruct((b, e, s), F32),
                   jax.ShapeDtypeStruct((b, s, LANES), F32)],
        compiler_params=_params(("arbitrary",)),
        name="router",
    )(logits)


def _gather_kernel(rank_ref, wval_ref, h2_ref, xg_ref, val_ref, *, cap):
    s = rank_ref.shape[3]
    r_iota = lax.broadcasted_iota(jnp.int32, (cap, s), 0).astype(F32)
    for k in range(GATHER_EXPERTS_PER_STEP):
        hot = rank_ref[0, k] == r_iota
        xg_ref[0, k] = jnp.dot(hot.astype(BF16), h2_ref[0],
                               preferred_element_type=F32).astype(BF16)
        val_ref[0, k] = jnp.sum(jnp.where(hot, wval_ref[0, k], 0.0), axis=1, keepdims=True)


def _gather(rank, wval, h2, cap):
    b, e, s = rank.shape
    d = h2.shape[2]
    ge = GATHER_EXPERTS_PER_STEP
    assert e % ge == 0
    r4 = rank.reshape(b, e, 1, s)
    w4 = wval.reshape(b, e, 1, s)
    row = pl.BlockSpec((1, ge, 1, s), lambda bi, ei: (bi, ei, 0, 0))
    return pl.pallas_call(
        functools.partial(_gather_kernel, cap=cap),
        grid=(b, e // ge),
        in_specs=[row, row, pl.BlockSpec((1, s, d), lambda bi, ei: (bi, 0, 0))],
        out_specs=[pl.BlockSpec((1, ge, cap, d), lambda bi, ei: (bi, ei, 0, 0)),
                   pl.BlockSpec((1, ge, cap, 1), lambda bi, ei: (bi, ei, 0, 0))],
        out_shape=[jax.ShapeDtypeStruct((b, e, cap, d), BF16),
                   jax.ShapeDtypeStruct((b, e, cap, 1), F32)],
        compiler_params=_params(("arbitrary", "arbitrary")),
        name="gather",
    )(r4, w4, h2)


def _expert_kernel(xg_ref, val_ref, wg_ref, wu_ref, wd_ref, y_ref, wg_s, wu_s, wd_s, *, n_exp):
    p = pl.program_id(0)
    bi = pl.program_id(1)
    stage = p % 2
    ready = 1 - stage

    def stage_chunk():
        for w_ref, w_s in ((wg_ref, wg_s), (wu_ref, wu_s), (wd_ref, wd_s)):
            rows = w_ref.shape[1]
            w_s[stage, pl.ds(pl.multiple_of(bi * rows, rows), rows), :] = w_ref[0].astype(BF16)

    def run_expert():
        nb, cap = xg_ref.shape[0], xg_ref.shape[2]
        xg = jnp.concatenate([xg_ref[k, 0] for k in range(nb)], axis=0)
        a = jnp.dot(xg, wg_s[ready], preferred_element_type=F32)
        u = jnp.dot(xg, wu_s[ready], preferred_element_type=F32)
        hm = (a * jax.nn.sigmoid(a) * u).astype(BF16)
        y = jnp.dot(hm, wd_s[ready], preferred_element_type=F32)
        for k in range(nb):
            y_ref[k, 0] = (y[k * cap:(k + 1) * cap] * val_ref[k, 0]).astype(BF16)

    pl.when(p == 0)(stage_chunk)

    @pl.when((p > 0) & (p < n_exp))
    def _():
        stage_chunk()
        run_expert()

    pl.when(p == n_exp)(run_expert)


def _experts(xg, val, wg, wu, wd):
    b, e, cap, d = xg.shape
    f = wg.shape[2]
    nbs = EXPERT_BATCHES_PER_STEP
    steps = b // nbs
    assert b % nbs == 0 and d % steps == 0 and f % steps == 0
    last = e - 1
    tok = lambda p, bi: (jnp.where(p == 0, 0, bi), jnp.maximum(p - 1, 0), 0, 0)
    chunk = lambda p, bi: (jnp.minimum(p, last), jnp.where(p <= last, bi, steps - 1), 0)
    return pl.pallas_call(
        functools.partial(_expert_kernel, n_exp=e),
        grid=(e + 1, steps),
        in_specs=[pl.BlockSpec((nbs, 1, cap, d), tok),
                  pl.BlockSpec((nbs, 1, cap, 1), tok),
                  pl.BlockSpec((1, d // steps, f), chunk),
                  pl.BlockSpec((1, d // steps, f), chunk),
                  pl.BlockSpec((1, f // steps, d), chunk)],
        out_specs=pl.BlockSpec((nbs, 1, cap, d), tok),
        out_shape=jax.ShapeDtypeStruct((b, e, cap, d), BF16),
        scratch_shapes=[pltpu.VMEM((2, d, f), BF16),
                        pltpu.VMEM((2, d, f), BF16),
                        pltpu.VMEM((2, f, d), BF16)],
        compiler_params=_params(("arbitrary", "arbitrary")),
        name="experts",
    )(xg, val, wg, wu, wd)


def _combine_kernel(rankc_ref, y_ref, x1_ref, mod_ref, gfin_ref, o_ref, *, cap, final):
    tm = rankc_ref.shape[1]
    rc = rankc_ref[0]
    col = lax.broadcasted_iota(jnp.int32, (tm, cap), 1).astype(F32)
    hot = jnp.concatenate(
        [(rc[:, e:e + 1] == col).astype(BF16) for e in range(N_EXPERTS)], axis=1)
    moe = jnp.dot(hot, y_ref[0], preferred_element_type=F32)
    x2 = x1_ref[0] + mod_ref[0, 5:6, :] * moe
    if final:
        x2 = _rms(x2) * gfin_ref[...]
    o_ref[0] = x2


def _combine(rankc, y, x1, mod, g_final, cap, tm, final):
    b, s, d = x1.shape
    y2 = y.reshape(b, N_EXPERTS * cap, d)
    tok = lambda width: pl.BlockSpec((1, tm, width), lambda bi, i: (bi, i, 0))
    return pl.pallas_call(
        functools.partial(_combine_kernel, cap=cap, final=final),
        grid=(b, s // tm),
        in_specs=[tok(LANES),
                  pl.BlockSpec((1, N_EXPERTS * cap, d), lambda bi, i: (bi, 0, 0)),
                  tok(d),
                  pl.BlockSpec((1, 6, d), lambda bi, i: (bi, 0, 0)),
                  pl.BlockSpec((1, d), lambda bi, i: (0, 0))],
        out_specs=tok(d),
        out_shape=jax.ShapeDtypeStruct((b, s, d), F32),
        compiler_params=_params(("arbitrary", "arbitrary")),
        name="combine",
    )(rankc, y2, x1, mod, g_final.reshape(1, d))


def _rope_tables(s):
    half = MLA_ROPE_DIM // 2
    f32 = np.float32
    freqs = f32(ROPE_THETA) ** (-np.arange(half, dtype=f32) / f32(half))
    t = np.arange(s)
    ang = lambda pos: pos.astype(f32)[:, None] * freqs[None, :]
    a1, ar, ac = ang(t), ang(t // GRID_W), ang(t % GRID_W)
    z = np.zeros((s, half), f32)
    cat = lambda *p: jnp.asarray(np.concatenate(p, axis=1), F32)
    c1 = cat(np.cos(a1), z, np.cos(a1), z)
    s1 = cat(-np.sin(a1), z, np.sin(a1), z)
    c2 = cat(np.cos(ar), np.cos(ac), np.cos(ar), np.cos(ac))
    s2 = cat(-np.sin(ar), -np.sin(ac), np.sin(ar), np.sin(ac))
    return c1, s1, c2, s2


def _pair_apart(w, heads):
    lead = w.shape[:-1]
    w = w.reshape(*lead, heads, 4, GQA_HEAD_DIM // 4)
    return w[..., jnp.array([0, 2, 1, 3]), :].reshape(*lead, heads * GQA_HEAD_DIM)


def _spread_rope(w):
    half = MLA_ROPE_DIM // 2
    z = jnp.zeros(w.shape[:-1] + (half,), w.dtype)
    return jnp.concatenate([w[..., :half], z, w[..., half:], z], axis=-1)


def _regroup_kernel(w_ref, o_ref, *, segments):
    quarter = GQA_HEAD_DIM // 4
    out = 0

    def move(src, rows):
        nonlocal out
        o_ref[out:out + rows, :] = w_ref[src:src + rows, :].astype(BF16)
        out += rows

    def zeros(rows):
        nonlocal out
        o_ref[out:out + rows, :] = jnp.zeros((rows, o_ref.shape[1]), BF16)
        out += rows

    for src, rows, kind in segments:
        if kind == "copy":
            move(src, rows)
        elif kind == "zero":
            zeros(rows)
        elif kind == "pair_apart":
            for head in range(0, rows, GQA_HEAD_DIM):
                for block in (0, 2, 1, 3):
                    move(src + head + block * quarter, quarter)
        else:
            half = rows // 2
            for part in range(2):
                move(src + part * half, half)
                zeros(half)
    assert out == o_ref.shape[0]


def _regroup_w_in(w_in):
    d, n = w_in.shape
    names = ("cq", "ckv", "kr", "qg", "kg", "vg", "ga", "gb")
    widths = (MLA_Q_RANK, MLA_KV_RANK, MLA_ROPE_DIM, GQA_HEADS * GQA_HEAD_DIM,
              GQA_KV_HEADS * GQA_HEAD_DIM, GQA_KV_HEADS * GQA_HEAD_DIM, d, d)
    src, off = {}, 0
    for name, w in zip(names, widths):
        src[name] = (off, w)
        off += w
    assert off == n
    segments = [src["ga"] + ("copy",), src["gb"] + ("copy",), src["qg"] + ("pair_apart",),
                src["kg"] + ("pair_apart",), src["vg"] + ("copy",), src["cq"] + ("copy",),
                src["ckv"] + ("copy",), src["kr"] + ("spread",)]
    used = sum(2 * w if kind == "spread" else w for _, w, kind in segments)
    segments.append((0, _Z_COLS - used, "zero"))
    cols = 256
    return pl.pallas_call(
        functools.partial(_regroup_kernel, segments=tuple(segments)),
        grid=(d // cols,),
        in_specs=[pl.BlockSpec((n, cols), lambda i: (0, i))],
        out_specs=pl.BlockSpec((_Z_COLS, cols), lambda i: (0, i)),
        out_shape=jax.ShapeDtypeStruct((_Z_COLS, d), BF16),
        compiler_params=_params(("arbitrary",)),
        name="regroup",
    )(jnp.swapaxes(w_in, 0, 1))


def _regroup_w_uq(w_uq):
    r = w_uq.shape[0]
    w = w_uq.astype(BF16).reshape(r, MLA_HEADS, MLA_NOPE_DIM + MLA_ROPE_DIM)
    w = jnp.concatenate([w[..., :MLA_NOPE_DIM], _spread_rope(w[..., MLA_NOPE_DIM:])], axis=-1)
    return w.reshape(r, MLA_HEADS * MLA_QK_PAD)


def _regroup_w_ukv(w_ukv):
    r = w_ukv.shape[0]
    w = w_ukv.reshape(r, MLA_HEADS, MLA_NOPE_DIM + MLA_V_DIM)
    wk = w[:, :, :MLA_NOPE_DIM].reshape(r, MLA_HEADS * MLA_NOPE_DIM)
    wv = w[:, :, MLA_NOPE_DIM:].reshape(r, MLA_HEADS * MLA_V_DIM)
    return jnp.concatenate([wk, wv], axis=1).astype(BF16)


def _split_router(w_router):
    d, e = w_router.shape
    w = jnp.pad(w_router, ((0, 0), (0, LANES - e)))
    hi = w.astype(BF16)
    lo = (w - hi.astype(F32)).astype(BF16)
    return jnp.concatenate([hi, lo], axis=1)


def kernel(x, c, w_ada, b_ada, g_mix, w_in, g_cq, g_ckv, w_uq, w_ukv, g_qn, g_kn, w_br_a, w_br_b, w_out, g_ffn, w_router, w_e_gate, w_e_up, w_e_down, g_final):
    b, s, d = x.shape
    depth = w_ada.shape[0]
    cap = EC_CAPACITY * s // N_EXPERTS
    assert d == _Z_GATE_B and s % LANES == 0 and cap % LANES == 0
    rows = lambda n: min(n, s)
    tabs = _rope_tables(s)
    for l in range(depth):
        mod = _adaln(c, w_ada[l], b_ada[l]).reshape(b, 6, d)
        z = _inproj(x, mod, g_mix[l], _regroup_w_in(w_in[l]), rows(INPROJ_ROWS))
        qm, kn, krr, vm, qb, kb = _prep(z, g_cq[l], g_ckv[l], _pair_apart(g_qn[l], 1),
                                        _pair_apart(g_kn[l], 1), _regroup_w_uq(w_uq[l]),
                                        _regroup_w_ukv(w_ukv[l]), tabs, rows(PREP_ROWS))
        o_a = _attention(qm, kn, krr, vm, groups=MLA_HEADS // MLA_HEADS_PER_STEP,
                         heads=MLA_HEADS_PER_STEP, dk=MLA_QK_PAD, dv=MLA_V_DIM, shared_kv=False,
                         v_block0=0, tq=rows(ATTN_Q_ROWS), name="mla_attn")
        o_b = _attention(qb, kb, None, z, groups=GQA_KV_HEADS, heads=GQA_GROUP, dk=GQA_HEAD_DIM,
                         dv=GQA_HEAD_DIM, shared_kv=True, v_block0=_Z_VG // GQA_HEAD_DIM,
                         tq=rows(ATTN_Q_ROWS), name="gqa_attn")
        wr = _split_router(w_router[l])
        x1, h2, logits = _merge(o_a, o_b, z, x, mod, g_ffn[l], w_br_a[l].astype(BF16),
                                w_br_b[l].astype(BF16), w_out[l].astype(BF16), wr,
                                rows(MERGE_ROWS))
        rank, wval, rankc = _router(logits, cap)
        xg, val = _gather(rank, wval, h2, cap)
        y = _experts(xg, val, w_e_gate[l], w_e_up[l], w_e_down[l])
        x = _combine(rankc, y, x1, mod, g_final, cap, rows(COMBINE_ROWS),
                     final=(l == depth - 1))
    return x
```
